```python
import math
import jax, jax.numpy as jnp
from jax import lax
import numpy as np

D_MODEL = 1024
BATCH = 16
SEQ = 2048
DEPTH = 1
DEC_BATCH = 128
DEC_SEQ = 4
PAST_LEN = 16384
PAGE_SIZE = 128

ATTN_WIDTH = D_MODEL // 2
SSM_WIDTH = D_MODEL - ATTN_WIDTH
HEAD_DIM = 64
N_HEADS = ATTN_WIDTH // HEAD_DIM
N_KV_HEADS = 2
GQA_GROUP = N_HEADS // N_KV_HEADS
KV_WIDTH = N_KV_HEADS * HEAD_DIM
WINDOW = 128
BLOCK = WINDOW
ROT_DIM = HEAD_DIM // 4
ROPE_THETA = 500000.0
SSM_GROUP = 16
N_SSM_GROUPS = SSM_WIDTH // SSM_GROUP
SSM_STATE = 64
DT_MIN = 1e-3
DT_MAX = 1e-1
D_FF = 4 * D_MODEL
IN_WIDTH = ATTN_WIDTH + 2 * KV_WIDTH + SSM_WIDTH
NORM_EPS = 1e-6
NEG_INF = -1e30

kernel_name = "hymba_swa_sink_s5_sqrelu_step"


def _rmsnorm(x, g):
    xf = x.astype(jnp.float32)
    y = xf * lax.rsqrt(jnp.mean(xf * xf, axis=-1, keepdims=True) + NORM_EPS)
    return (y * g.astype(jnp.float32)).astype(x.dtype)


def _rotary(x, pos):
    half = ROT_DIM // 2
    inv_freq = jnp.power(ROPE_THETA, -jnp.arange(half, dtype=jnp.float32) * (2.0 / ROT_DIM))
    ang = pos.astype(jnp.float32)[:, None] * inv_freq[None, :]
    cos = jnp.cos(ang)[:, None, :]
    sin = jnp.sin(ang)[:, None, :]
    xr = x[..., :ROT_DIM].astype(jnp.float32)
    x1, x2 = xr[..., :half], xr[..., half:]
    rot = jnp.concatenate([x1 * cos - x2 * sin, x2 * cos + x1 * sin], axis=-1).astype(x.dtype)
    return jnp.concatenate([rot, x[..., ROT_DIM:]], axis=-1)


def _sink_attention(q, k, v, mask, sinks):
    s = jnp.einsum('...qkgd,...skd->...kgqs', q, k,
                   preferred_element_type=jnp.float32) * (HEAD_DIM ** -0.5)
    s = jnp.where(mask, s, NEG_INF)
    sink = sinks.astype(jnp.float32).reshape(N_KV_HEADS, GQA_GROUP, 1, 1)
    m = jnp.maximum(jnp.max(s, axis=-1, keepdims=True), sink)
    p = jnp.exp(s - m)
    denom = jnp.sum(p, axis=-1, keepdims=True) + jnp.exp(sink - m)
    return jnp.einsum('...kgqs,...skd->...qkgd', (p / denom).astype(v.dtype), v)


def _banded_attention(q, k, v, sinks):
    B_, L = q.shape[0], q.shape[1]
    nb = L // BLOCK
    qb = q.reshape(B_, nb, BLOCK, N_KV_HEADS, GQA_GROUP, HEAD_DIM)

    def with_prev(t):
        tb = t.reshape(B_, nb, BLOCK, N_KV_HEADS, HEAD_DIM)
        prev = jnp.concatenate([jnp.zeros_like(tb[:, :1]), tb[:, :-1]], axis=1)
        return jnp.concatenate([prev, tb], axis=2)

    kk, vv = with_prev(k), with_prev(v)
    i = jnp.arange(BLOCK)[:, None]
    j = jnp.arange(2 * BLOCK)[None, :]
    diff = i + BLOCK - j
    band = (diff >= 0) & (diff < WINDOW)
    kpos = jnp.arange(nb)[:, None, None] * BLOCK - BLOCK + j[None]
    mask = (band[None] & (kpos >= 0))[None, :, None, None]
    o = _sink_attention(qb, kk, vv, mask, sinks)
    return o.reshape(B_, L, ATTN_WIDTH)


def _window_step_attention(q, k_new, v_new, cache_k, cache_v, sinks):
    Bd, S = q.shape[0], q.shape[1]
    W = cache_k.shape[1]
    kk = jnp.concatenate([cache_k.astype(k_new.dtype), k_new], axis=1)
    vv = jnp.concatenate([cache_v.astype(v_new.dtype), v_new], axis=1)
    i = jnp.arange(S)[:, None]
    j = jnp.arange(W + S)[None, :]
    diff = i + W - j
    mask = ((diff >= 0) & (diff < WINDOW))[None, None, None]
    qg = q.reshape(Bd, S, N_KV_HEADS, GQA_GROUP, HEAD_DIM)
    o = _sink_attention(qg, kk, vv, mask, sinks)
    return o.reshape(Bd, S, ATTN_WIDTH), kk[:, -W:], vv[:, -W:]


def _ssm_mixer(u, s0_re, s0_im, a_re, a_im, log_dt, b_re, b_im, c_re, c_im, d_skip, w_glu, b_glu):
    f32 = jnp.float32
    B_, L = u.shape[0], u.shape[1]
    ar, ai = a_re.astype(f32), a_im.astype(f32)
    dt = jnp.exp(log_dt.astype(f32))[:, None]
    mag = jnp.exp(ar * dt)
    ab_re, ab_im = mag * jnp.cos(ai * dt), mag * jnp.sin(ai * dt)
    den = ar * ar + ai * ai
    nr, ni = ab_re - 1.0, ab_im
    f_re = (nr * ar + ni * ai) / den
    f_im = (ni * ar - nr * ai) / den
    br, bi = b_re.astype(f32), b_im.astype(f32)
    bb_re = f_re[..., None] * br - f_im[..., None] * bi
    bb_im = f_re[..., None] * bi + f_im[..., None] * br

    uf = u.astype(f32).reshape(B_, L, N_SSM_GROUPS, SSM_GROUP)
    bu_re = jnp.einsum('blgc,gpc->blgp', uf, bb_re)
    bu_im = jnp.einsum('blgc,gpc->blgp', uf, bb_im)
    shape_a = (1, L, N_SSM_GROUPS, SSM_STATE)
    a_seq_re = jnp.broadcast_to(ab_re[None, None], shape_a)
    a_seq_im = jnp.broadcast_to(ab_im[None, None], shape_a)

    def combine(e1, e2):
        a1r, a1i, b1r, b1i = e1
        a2r, a2i, b2r, b2i = e2
        return (a1r * a2r - a1i * a2i,
                a1r * a2i + a1i * a2r,
                a2r * b1r - a2i * b1i + b2r,
                a2r * b1i + a2i * b1r + b2i)

    acr, aci, xr, xi = lax.associative_scan(combine, (a_seq_re, a_seq_im, bu_re, bu_im), axis=1)
    if s0_re is not None:
        s0r = s0_re.astype(f32)[:, None]
        s0i = s0_im.astype(f32)[:, None]
        xr, xi = xr + acr * s0r - aci * s0i, xi + acr * s0i + aci * s0r
    y = (jnp.einsum('blgp,gcp->blgc', xr, c_re.astype(f32))
         - jnp.einsum('blgp,gcp->blgc', xi, c_im.astype(f32)))
    y = y.reshape(B_, L, SSM_WIDTH) + d_skip.astype(f32) * u.astype(f32)
    g = jax.nn.gelu(y)
    g = g * jax.nn.sigmoid(g @ w_glu.astype(f32) + b_glu.astype(f32))
    return g.astype(u.dtype), xr[:, -1], xi[:, -1]


def _layer(x, pos, cache_k, cache_v, s0_re, s0_im,
           g_mix_pre, w_in, sinks, a_re, a_im, log_dt, b_re, b_im, c_re, c_im, d_skip,
           w_glu, b_glu, w_out, g_mix_post, g_mlp_pre, w_up, w_down, g_mlp_post):
    B_, L = x.shape[0], x.shape[1]
    h = _rmsnorm(x, g_mix_pre)
    proj = h @ w_in
    q, k, v, u = jnp.split(proj, [ATTN_WIDTH, ATTN_WIDTH + KV_WIDTH, ATTN_WIDTH + 2 * KV_WIDTH], axis=-1)
    q = _rotary(q.reshape(B_, L, N_HEADS, HEAD_DIM), pos)
    k = _rotary(k.reshape(B_, L, N_KV_HEADS, HEAD_DIM), pos)
    v = v.reshape(B_, L, N_KV_HEADS, HEAD_DIM)
    if cache_k is None:
        att = _banded_attention(q, k, v, sinks)
        keep = min(WINDOW, L)
        k_state, v_state = k[:, -keep:], v[:, -keep:]
    else:
        att, k_state, v_state = _window_step_attention(q, k, v, cache_k, cache_v, sinks)
    ssm, s_re, s_im = _ssm_mixer(u, s0_re, s0_im, a_re, a_im, log_dt, b_re, b_im,
                                 c_re, c_im, d_skip, w_glu, b_glu)
    mix = jnp.concatenate([att, ssm], axis=-1) @ w_out
    x = x + _rmsnorm(mix, g_mix_post)
    h = _rmsnorm(x, g_mlp_pre)
    f = jnp.square(jax.nn.relu(h @ w_up)) @ w_down
    x = x + _rmsnorm(f, g_mlp_post)
    return x, k_state, v_state, s_re, s_im


def setup_inputs(seed: int = 0) -> dict:
    key = jax.random.key(seed)
    ks = jax.random.split(key, 32)
    f32 = jnp.float32
    win = min(WINDOW, PAST_LEN)
    nrm = lambda k, shape, scale: jax.random.normal(k, shape, f32) * scale
    n_idx = jnp.arange(SSM_STATE, dtype=f32)
    return {
        "x_prompt": nrm(ks[0], (BATCH, SEQ, D_MODEL), 1.0),
        "x_sample": nrm(ks[1], (DEC_BATCH, DEC_SEQ, D_MODEL), 1.0),
        "cache_k": nrm(ks[2], (DEPTH, DEC_BATCH, win, N_KV_HEADS, HEAD_DIM), 1.0),
        "cache_v": nrm(ks[3], (DEPTH, DEC_BATCH, win, N_KV_HEADS, HEAD_DIM), 1.0),
        "state_ssm_re": nrm(ks[4], (DEPTH, DEC_BATCH, N_SSM_GROUPS, SSM_STATE), 0.1),
        "state_ssm_im": nrm(ks[5], (DEPTH, DEC_BATCH, N_SSM_GROUPS, SSM_STATE), 0.1),
        "g_mix_pre": 1.0 + nrm(ks[6], (DEPTH, D_MODEL), 0.01),
        "w_in": nrm(ks[7], (DEPTH, D_MODEL, IN_WIDTH), D_MODEL ** -0.5),
        "sinks": nrm(ks[8], (DEPTH, N_HEADS), 0.5),
        "a_re": -0.5 + nrm(ks[9], (DEPTH, N_SSM_GROUPS, SSM_STATE), 0.01),
        "a_im": math.pi * n_idx[None, None, :] + nrm(ks[10], (DEPTH, N_SSM_GROUPS, SSM_STATE), 0.01),
        "log_dt": jax.random.uniform(ks[11], (DEPTH, N_SSM_GROUPS), f32,
                                     math.log(DT_MIN), math.log(DT_MAX)),
        "b_re": nrm(ks[12], (DEPTH, N_SSM_GROUPS, SSM_STATE, SSM_GROUP), (2 * SSM_GROUP) ** -0.5),
        "b_im": nrm(ks[13], (DEPTH, N_SSM_GROUPS, SSM_STATE, SSM_GROUP), (2 * SSM_GROUP) ** -0.5),
        "c_re": nrm(ks[14], (DEPTH, N_SSM_GROUPS, SSM_GROUP, SSM_STATE), (2 * SSM_STATE) ** -0.5),
        "c_im": nrm(ks[15], (DEPTH, N_SSM_GROUPS, SSM_GROUP, SSM_STATE), (2 * SSM_STATE) ** -0.5),
        "d_skip": nrm(ks[16], (DEPTH, SSM_WIDTH), 1.0),
        "w_glu": nrm(ks[17], (DEPTH, SSM_WIDTH, SSM_WIDTH), SSM_WIDTH ** -0.5),
        "b_glu": nrm(ks[18], (DEPTH, SSM_WIDTH), 0.01),
        "w_out": nrm(ks[19], (DEPTH, D_MODEL, D_MODEL), D_MODEL ** -0.5),
        "g_mix_post": 1.0 + nrm(ks[20], (DEPTH, D_MODEL), 0.01),
        "g_mlp_pre": 1.0 + nrm(ks[21], (DEPTH, D_MODEL), 0.01),
        "w_up": nrm(ks[22], (DEPTH, D_MODEL, D_FF), D_MODEL ** -0.5),
        "w_down": nrm(ks[23], (DEPTH, D_FF, D_MODEL), D_FF ** -0.5),
        "g_mlp_post": 1.0 + nrm(ks[24], (DEPTH, D_MODEL), 0.01),
    }


def reference(x_prompt, x_sample, cache_k, cache_v, state_ssm_re, state_ssm_im,
              g_mix_pre, w_in, sinks, a_re, a_im, log_dt, b_re, b_im, c_re, c_im, d_skip,
              w_glu, b_glu, w_out, g_mix_post, g_mlp_pre, w_up, w_down, g_mlp_post):
    pos_prompt = jnp.arange(x_prompt.shape[1], dtype=jnp.int32)
    pos_sample = PAST_LEN + jnp.arange(x_sample.shape[1], dtype=jnp.int32)
    yp, ys = x_prompt, x_sample
    kp_l, vp_l, srp_l, sip_l = [], [], [], []
    ks_l, vs_l, srs_l, sis_l = [], [], [], []
    for l in range(DEPTH):
        w = (g_mix_pre[l], w_in[l], sinks[l], a_re[l], a_im[l], log_dt[l], b_re[l], b_im[l],
             c_re[l], c_im[l], d_skip[l], w_glu[l], b_glu[l], w_out[l], g_mix_post[l],
             g_mlp_pre[l], w_up[l], w_down[l], g_mlp_post[l])
        yp, kp, vp, srp, sip = _layer(yp, pos_prompt, None, None, None, None, *w)
        ys, kss, vss, srs, sis = _layer(ys, pos_sample, cache_k[l], cache_v[l],
                                        state_ssm_re[l], state_ssm_im[l], *w)
        kp_l.append(kp); vp_l.append(vp); srp_l.append(srp); sip_l.append(sip)
        ks_l.append(kss); vs_l.append(vss); srs_l.append(srs); sis_l.append(sis)
    k_win_prompt = jnp.stack(kp_l)
    v_win_prompt = jnp.stack(vp_l)
    ssm_re_prompt = jnp.stack(srp_l)
    ssm_im_prompt = jnp.stack(sip_l)
    k_win_sample = jnp.stack(ks_l)
    v_win_sample = jnp.stack(vs_l)
    ssm_re_sample = jnp.stack(srs_l)
    ssm_im_sample = jnp.stack(sis_l)
    return (yp, ys, k_win_prompt, v_win_prompt, ssm_re_prompt, ssm_im_prompt,
            k_win_sample, v_win_sample, ssm_re_sample, ssm_im_sample)
```

```python
import functools
import math

import jax
import jax.numpy as jnp
from jax import lax
from jax.experimental import pallas as pl
from jax.experimental.pallas import tpu as pltpu

HEAD_DIM = 64
N_KV_HEADS = 2
WINDOW = 128
ROT_DIM = HEAD_DIM // 4
ROPE_THETA = 500000.0
SSM_GROUP = 16
SSM_STATE = 64
NORM_EPS = 1e-6
NEG_INF = -1e30
PAST_LEN = 16384
CHUNK = 128
LANES = 128
VMEM_LIMIT = 56 * 1024 * 1024

_F32 = jnp.float32
_BF16 = jnp.bfloat16


def _params(n_parallel=1):
    return pltpu.CompilerParams(dimension_semantics=("arbitrary",) * n_parallel,
                                vmem_limit_bytes=VMEM_LIMIT)


def _rms(x, g):
    return x * lax.rsqrt(jnp.mean(x * x, axis=-1, keepdims=True) + NORM_EPS) * g


def _const_spec(shape):
    nd = len(shape)
    return pl.BlockSpec(shape, lambda *_: (0,) * nd)


def _rotary_tables(pos):
    half = ROT_DIM // 2
    inv_freq = jnp.power(ROPE_THETA, -jnp.arange(half, dtype=_F32) * (2.0 / ROT_DIM))
    ang = pos.astype(_F32)[:, None] * inv_freq[None, :]
    cos, sin = jnp.cos(ang), jnp.sin(ang)
    n = pos.shape[0]
    ones = jnp.ones((n, HEAD_DIM - ROT_DIM), _F32)
    zeros_h = jnp.zeros((n, half), _F32)
    zeros_r = jnp.zeros((n, HEAD_DIM - ROT_DIM), _F32)
    c_tab = jnp.concatenate([cos, cos, ones], axis=1)
    s_up = jnp.concatenate([-sin, zeros_h, zeros_r], axis=1)
    s_dn = jnp.concatenate([zeros_h, sin, zeros_r], axis=1)
    rep = LANES // HEAD_DIM
    return tuple(jnp.tile(t, (1, rep)) for t in (c_tab, s_up, s_dn))


def _inproj_kernel(x_ref, g_ref, wqkv_ref, wu_ref, c_ref, su_ref, sd_ref,
                   q_ref, k_ref, v_ref, u_ref, *, attn_width, kv_width, u_transposed):
    half = ROT_DIM // 2
    h = _rms(x_ref[...], g_ref[...]).astype(_BF16)
    qkv = jnp.dot(h, wqkv_ref[...], preferred_element_type=_F32)
    c_tab, s_up, s_dn = c_ref[...], su_ref[...], sd_ref[...]

    def rot(blk):
        return (blk * c_tab + pltpu.roll(blk, LANES - half, axis=1) * s_up
                + pltpu.roll(blk, half, axis=1) * s_dn)

    for j in range(attn_width // LANES):
        q_ref[:, j * LANES:(j + 1) * LANES] = rot(qkv[:, j * LANES:(j + 1) * LANES]).astype(q_ref.dtype)
    for j in range(kv_width // LANES):
        o = attn_width + j * LANES
        k_ref[:, j * LANES:(j + 1) * LANES] = rot(qkv[:, o:o + LANES])
    v_ref[...] = qkv[:, attn_width + kv_width:]
    if u_transposed:
        u = lax.dot_general(wu_ref[...], h, (((1,), (1,)), ((), ())), preferred_element_type=_F32)
    else:
        u = jnp.dot(h, wu_ref[...], preferred_element_type=_F32)
    u_ref[...] = u.astype(u_ref.dtype)


def _inproj(x2, g, wqkv, wu, tables, *, tm, u_transposed, q_dtype, u_dtype):
    n, d = x2.shape
    kv_width = N_KV_HEADS * HEAD_DIM
    attn_width = wqkv.shape[1] - 2 * kv_width
    ssm_width = wu.shape[0] if u_transposed else wu.shape[1]
    nt = n // tm
    tab_blocks = tables[0].shape[0] // tm
    tab_spec = pl.BlockSpec((tm, LANES), lambda i: (i % tab_blocks, 0))
    if u_transposed:
        u_shape, u_spec = (ssm_width, n), pl.BlockSpec((ssm_width, tm), lambda i: (0, i))
    else:
        u_shape, u_spec = (n, ssm_width), pl.BlockSpec((tm, ssm_width), lambda i: (i, 0))
    return pl.pallas_call(
        functools.partial(_inproj_kernel, attn_width=attn_width, kv_width=kv_width,
                          u_transposed=u_transposed),
        grid=(nt,),
        in_specs=[pl.BlockSpec((tm, d), lambda i: (i, 0)), _const_spec((1, d)),
                  _const_spec(wqkv.shape), _const_spec(wu.shape), tab_spec, tab_spec, tab_spec],
        out_specs=[pl.BlockSpec((tm, attn_width), lambda i: (i, 0)),
                   pl.BlockSpec((tm, kv_width), lambda i: (i, 0)),
                   pl.BlockSpec((tm, kv_width), lambda i: (i, 0)), u_spec],
        out_shape=[jax.ShapeDtypeStruct((n, attn_width), q_dtype),
                   jax.ShapeDtypeStruct((n, kv_width), _F32),
                   jax.ShapeDtypeStruct((n, kv_width), _F32),
                   jax.ShapeDtypeStruct(u_shape, u_dtype)],
        compiler_params=_params(),
        name="inproj_t" if u_transposed else "inproj_n",
    )(x2, g.reshape(1, d), wqkv, wu, *tables)


def _softmax_sink(s_parts, sink):
    m = sink
    for s in s_parts:
        m = jnp.maximum(m, jnp.max(s, axis=-1, keepdims=True))
    p_parts = [jnp.exp(s - m) for s in s_parts]
    denom = jnp.exp(sink - m)
    for p in p_parts:
        denom = denom + jnp.sum(p, axis=-1, keepdims=True)
    return [p / denom for p in p_parts]


def _attn_prompt_kernel(sink_ref, q_ref, kp_ref, kc_ref, vp_ref, vc_ref, o_ref, *, n_heads):
    blk = q_ref.shape[0]
    i_blk = pl.program_id(1)
    group = n_heads // N_KV_HEADS
    row = lax.broadcasted_iota(jnp.int32, (blk, 2 * blk), 0)
    col = lax.broadcasted_iota(jnp.int32, (blk, 2 * blk), 1)
    diff = row + blk - col
    mask = (diff >= 0) & (diff < WINDOW) & ((col >= blk) | (i_blk > 0))
    scale = HEAD_DIM ** -0.5
    for kv in range(N_KV_HEADS):
        ls = slice(kv * HEAD_DIM, (kv + 1) * HEAD_DIM)
        kk = jnp.concatenate([kp_ref[:, ls], kc_ref[:, ls]], axis=0).astype(_BF16)
        vv = jnp.concatenate([vp_ref[:, ls], vc_ref[:, ls]], axis=0).astype(_BF16)
        for gi in range(group):
            hd = kv * group + gi
            qh = q_ref[:, hd * HEAD_DIM:(hd + 1) * HEAD_DIM]
            s = lax.dot_general(qh, kk, (((1,), (1,)), ((), ())), preferred_element_type=_F32) * scale
            s = jnp.where(mask, s, NEG_INF)
            (p,) = _softmax_sink([s], sink_ref[hd])
            o = jnp.dot(p.astype(_BF16), vv, preferred_element_type=_F32)
            o_ref[:, hd * HEAD_DIM:(hd + 1) * HEAD_DIM] = o.astype(o_ref.dtype)


def _attn_prompt(q3, k3, v3, sinks):
    b, l, aw = q3.shape
    kvw = k3.shape[2]
    nb = l // WINDOW
    cur = lambda bi, i: (bi, i, 0)
    prev = lambda bi, i: (bi, jnp.maximum(i - 1, 0), 0)
    return pl.pallas_call(
        functools.partial(_attn_prompt_kernel, n_heads=aw // HEAD_DIM),
        grid=(b, nb),
        in_specs=[pl.BlockSpec(memory_space=pltpu.SMEM),
                  pl.BlockSpec((None, WINDOW, aw), cur),
                  pl.BlockSpec((None, WINDOW, kvw), prev), pl.BlockSpec((None, WINDOW, kvw), cur),
                  pl.BlockSpec((None, WINDOW, kvw), prev), pl.BlockSpec((None, WINDOW, kvw), cur)],
        out_specs=pl.BlockSpec((None, WINDOW, aw), cur),
        out_shape=jax.ShapeDtypeStruct((b, l, aw), _BF16),
        compiler_params=_params(2),
        name="attn_prompt",
    )(sinks, q3, k3, k3, v3, v3)


def _cmul(ar, ai, br, bi):
    return ar * br - ai * bi, ar * bi + ai * br


def _discretise(ar, ai, dt):
    mag = jnp.exp(ar * dt)
    ab_re, ab_im = mag * jnp.cos(ai * dt), mag * jnp.sin(ai * dt)
    den = ar * ar + ai * ai
    nr, ni = ab_re - 1.0, ab_im
    f_re = (nr * ar + ni * ai) / den
    f_im = (ni * ar - nr * ai) / den
    return ab_re, ab_im, f_re, f_im


def _cpow(ar, ai, dt, tau):
    mag = jnp.exp(ar * dt * tau)
    ang = ai * dt * tau
    return mag * jnp.cos(ang), mag * jnp.sin(ang)


def _ssm_prep_kernel(ldt_ref, arow_re_ref, arow_im_ref, acol_re_ref, acol_im_ref,
                     bt_re_ref, bt_im_ref, c_re_ref, c_im_ref, ct_re_ref, ct_im_ref,
                     m_ref, bs_ref, ca_ref, ab_ref, apow_ref, bbt_ref, kt_ref):
    t_len, p_len, c_len = CHUNK, SSM_STATE, SSM_GROUP
    dt = jnp.exp(ldt_ref[...])
    ar, ai = arow_re_ref[...], arow_im_ref[...]
    ac, aic = acol_re_ref[...], acol_im_ref[...]
    ab_re, ab_im, f_re, f_im = _discretise(ar, ai, dt)
    ab_ref[...] = jnp.concatenate([ab_re, ab_im], axis=1)
    bb_re, bb_im = _cmul(f_re, f_im, bt_re_ref[...], bt_im_ref[...])
    bbt_ref[...] = jnp.concatenate([bb_re, bb_im], axis=1)

    pr, pi = _cpow(ar, ai, dt, float(t_len))
    rows = []
    for _ in range(apow_ref.shape[0]):
        rows.append(jnp.concatenate([pr, pi], axis=1))
        pr, pi = _cmul(pr, pi, pr, pi)
    apow_ref[...] = jnp.concatenate(rows, axis=0)

    tau_l = lax.broadcasted_iota(jnp.int32, (p_len, t_len), 1).astype(_F32)
    wt_re, wt_im = _cpow(ac, aic, dt, tau_l)
    c_re, c_im = c_re_ref[...], c_im_ref[...]
    g_rows = []
    for cp in range(c_len):
        g_re, g_im = _cmul(bb_re[cp:cp + 1, :], bb_im[cp:cp + 1, :], c_re, c_im)
        g_rows.append(jnp.concatenate([g_re, -g_im], axis=1))
    g2 = jnp.concatenate(g_rows, axis=0)
    wt = jnp.concatenate([wt_re, wt_im], axis=0)
    kt_ref[...] = jnp.dot(g2, wt, preferred_element_type=_F32, precision=lax.Precision.HIGHEST)

    s_idx = lax.broadcasted_iota(jnp.int32, (t_len, t_len), 0)
    t_idx = lax.broadcasted_iota(jnp.int32, (t_len, t_len), 1)
    causal = t_idx >= s_idx

    def toeplitz_rows(cp, carry):
        for c in range(c_len):
            taps = kt_ref[pl.ds(cp * c_len + c, 1), :]
            full = jnp.broadcast_to(taps, (t_len, t_len))
            skew = pltpu.roll(full, 0, axis=1, stride=1, stride_axis=0)
            m_ref[pl.ds(pl.multiple_of(cp * t_len, t_len), t_len), c * t_len:(c + 1) * t_len] = (
                jnp.where(causal, skew, 0.0).astype(m_ref.dtype))
        return carry

    lax.fori_loop(0, c_len, toeplitz_rows, 0)

    tau_s = (t_len - 1 - lax.broadcasted_iota(jnp.int32, (t_len, p_len), 0)).astype(_F32)
    w2_re, w2_im = _cpow(ar, ai, dt, tau_s)
    for cp in range(c_len):
        s_re, s_im = _cmul(w2_re, w2_im, bb_re[cp:cp + 1, :], bb_im[cp:cp + 1, :])
        bs_ref[cp * t_len:(cp + 1) * t_len, :] = jnp.concatenate([s_re, s_im], axis=1).astype(bs_ref.dtype)

    w3_re, w3_im = _cpow(ac, aic, dt, tau_l + 1.0)
    ct_re, ct_im = ct_re_ref[...], ct_im_ref[...]
    for c in range(c_len):
        y_re, y_im = _cmul(ct_re[:, c:c + 1], ct_im[:, c:c + 1], w3_re, w3_im)
        ca_ref[:, c * t_len:(c + 1) * t_len] = jnp.concatenate([y_re, -y_im], axis=0).astype(ca_ref.dtype)


def _ssm_prep(a_re, a_im, log_dt, b_re, b_im, c_re, c_im, n_pow):
    g, p = a_re.shape
    c = b_re.shape[2]
    tw = c * CHUNK
    grp = lambda *shape: pl.BlockSpec((None,) + shape, lambda i: (i,) + (0,) * len(shape))
    return pl.pallas_call(
        _ssm_prep_kernel,
        grid=(g,),
        in_specs=[grp(1, 1), grp(1, p), grp(1, p), grp(p, 1), grp(p, 1),
                  grp(c, p), grp(c, p), grp(c, p), grp(c, p), grp(p, c), grp(p, c)],
        out_specs=[grp(tw, tw), grp(tw, 2 * p), grp(2 * p, tw), grp(1, 2 * p), grp(n_pow, 2 * p),
                   grp(c, 2 * p)],
        out_shape=[jax.ShapeDtypeStruct((g, tw, tw), _BF16),
                   jax.ShapeDtypeStruct((g, tw, 2 * p), _BF16),
                   jax.ShapeDtypeStruct((g, 2 * p, tw), _BF16),
                   jax.ShapeDtypeStruct((g, 1, 2 * p), _F32),
                   jax.ShapeDtypeStruct((g, n_pow, 2 * p), _F32),
                   jax.ShapeDtypeStruct((g, c, 2 * p), _F32)],
        scratch_shapes=[pltpu.VMEM((c * c, CHUNK), _F32)],
        compiler_params=_params(),
        name="ssm_prep",
    )(log_dt.reshape(g, 1, 1), a_re.reshape(g, 1, p), a_im.reshape(g, 1, p),
      a_re.reshape(g, p, 1), a_im.reshape(g, p, 1),
      jnp.swapaxes(b_re, 1, 2), jnp.swapaxes(b_im, 1, 2), c_re, c_im,
      jnp.swapaxes(c_re, 1, 2), jnp.swapaxes(c_im, 1, 2))


def _cmul_packed(a_row, s):
    p = a_row.shape[1] // 2
    a_rr = jnp.concatenate([a_row[:, :p], a_row[:, :p]], axis=1)
    a_ii = jnp.concatenate([-a_row[:, p:], a_row[:, p:]], axis=1)
    return s * a_rr + pltpu.roll(s, p, axis=1) * a_ii


def _ssm_prompt_kernel(d_ref, u_ref, m_ref, bs_ref, ca_ref, apow_ref, g_ref, st_ref, *, chunks_per_seq):
    c_len = u_ref.shape[0]
    nc = u_ref.shape[1]
    grp = pl.program_id(0)
    x = jnp.concatenate([u_ref[c] for c in range(c_len)], axis=1)
    y = jnp.dot(x, m_ref[...], preferred_element_type=_F32)
    s = jnp.dot(x, bs_ref[...], preferred_element_type=_F32)
    k_idx = lax.broadcasted_iota(jnp.int32, s.shape, 0) % chunks_per_seq
    for j in range(apow_ref.shape[0]):
        d = 1 << j
        shifted = jnp.where(k_idx >= d, pltpu.roll(s, d, axis=0), 0.0)
        s = s + _cmul_packed(apow_ref[j:j + 1, :], shifted)
    st_ref[...] = s
    s_prev = jnp.where(k_idx >= 1, pltpu.roll(s, 1, axis=0), 0.0)
    y = y + jnp.dot(s_prev.astype(_BF16), ca_ref[...], preferred_element_type=_F32)
    for c in range(c_len):
        yc = y[:, c * CHUNK:(c + 1) * CHUNK] + d_ref[grp * c_len + c] * u_ref[c].astype(_F32)
        g_ref[c] = jax.nn.gelu(yc).astype(g_ref.dtype)


def _ssm_prompt(u_t3, d_skip, m, bs, ca, apow, chunks_per_seq):
    width, nc, t = u_t3.shape
    g, tw, _ = m.shape
    c = width // g
    p2 = bs.shape[2]
    n_pow = apow.shape[1]
    grp = lambda *shape: pl.BlockSpec((None,) + shape, lambda i: (i,) + (0,) * len(shape))
    return pl.pallas_call(
        functools.partial(_ssm_prompt_kernel, chunks_per_seq=chunks_per_seq),
        grid=(g,),
        in_specs=[pl.BlockSpec(memory_space=pltpu.SMEM),
                  pl.BlockSpec((c, nc, t), lambda i: (i, 0, 0)),
                  grp(tw, tw), grp(tw, p2), grp(p2, tw), grp(n_pow, p2)],
        out_specs=[pl.BlockSpec((c, nc, t), lambda i: (i, 0, 0)), grp(nc, p2)],
        out_shape=[jax.ShapeDtypeStruct((width, nc, t), _BF16),
                   jax.ShapeDtypeStruct((g, nc, p2), _F32)],
        compiler_params=_params(),
        name="ssm_prompt",
    )(d_skip, u_t3, m, bs, ca, apow)


def _mix_mlp_kernel(x_ref, att_ref, g_ref, wglu_ref, bglu_ref, wout_ref, gpost_ref, gpre_ref,
                    wup_ref, wdown_ref, gmlp_ref, o_ref, *, g_transposed, ff_chunk):
    aw = att_ref.shape[1]
    gact = g_ref[...]
    if g_transposed:
        z = jnp.dot(wglu_ref[...], gact, preferred_element_type=_F32) + bglu_ref[...]
    else:
        z = jnp.dot(gact.astype(_BF16), wglu_ref[...], preferred_element_type=_F32) + bglu_ref[...]
    ssm = (gact.astype(_F32) * (1.0 / (1.0 + jnp.exp(-z)))).astype(_BF16)
    mix = jnp.dot(att_ref[...], wout_ref[:aw, :], preferred_element_type=_F32)
    if g_transposed:
        mix = mix + lax.dot_general(ssm, wout_ref[aw:, :], (((0,), (0,)), ((), ())),
                                    preferred_element_type=_F32)
    else:
        mix = mix + jnp.dot(ssm, wout_ref[aw:, :], preferred_element_type=_F32)
    x1 = x_ref[...] + _rms(mix, gpost_ref[...])
    h = _rms(x1, gpre_ref[...]).astype(_BF16)
    d_ff = wup_ref.shape[1]
    f = None
    for j in range(d_ff // ff_chunk):
        up = jnp.dot(h, wup_ref[:, j * ff_chunk:(j + 1) * ff_chunk], preferred_element_type=_F32)
        act = jnp.square(jnp.maximum(up, 0.0)).astype(_BF16)
        part = jnp.dot(act, wdown_ref[j * ff_chunk:(j + 1) * ff_chunk, :], preferred_element_type=_F32)
        f = part if f is None else f + part
    o_ref[...] = x1 + _rms(f, gmlp_ref[...])


def _mix_mlp(x2, att, gact, wglu, bglu, wout, gpost, gpre, wup, wdown, gmlp, *, tm, g_transposed):
    n, d = x2.shape
    aw = att.shape[1]
    sw = wglu.shape[0]
    d_ff = wup.shape[1]
    if g_transposed:
        g_spec = pl.BlockSpec((sw, tm), lambda i: (0, i))
        bglu2 = bglu.reshape(sw, 1)
    else:
        g_spec = pl.BlockSpec((tm, sw), lambda i: (i, 0))
        bglu2 = bglu.reshape(1, sw)
    return pl.pallas_call(
        functools.partial(_mix_mlp_kernel, g_transposed=g_transposed, ff_chunk=min(d_ff, 1024)),
        grid=(n // tm,),
        in_specs=[pl.BlockSpec((tm, d), lambda i: (i, 0)), pl.BlockSpec((tm, aw), lambda i: (i, 0)), g_spec,
                  _const_spec(wglu.shape), _const_spec(bglu2.shape), _const_spec(wout.shape),
                  _const_spec((1, d)), _const_spec((1, d)), _const_spec(wup.shape),
                  _const_spec(wdown.shape), _const_spec((1, d))],
        out_specs=pl.BlockSpec((tm, d), lambda i: (i, 0)),
        out_shape=jax.ShapeDtypeStruct((n, d), _F32),
        compiler_params=_params(),
        name="mix_mlp_t" if g_transposed else "mix_mlp_n",
    )(x2, att, gact, wglu, bglu2, wout, gpost.reshape(1, d), gpre.reshape(1, d), wup, wdown,
      gmlp.reshape(1, d))


def _attn_sample_kernel(sink_ref, q_ref, kn_ref, vn_ref, kc_ref, vc_ref, o_ref, *, n_heads):
    steps, bb, _ = q_ref.shape
    win = kc_ref.shape[0] // bb
    group = n_heads // N_KV_HEADS
    rows = group * steps * bb
    r_c = lax.broadcasted_iota(jnp.int32, (rows, bb * win), 0)
    c_c = lax.broadcasted_iota(jnp.int32, (rows, bb * win), 1)
    t_r, b_r = (r_c % (steps * bb)) // bb, r_c % bb
    mask_c = (c_c // win == b_r) & (c_c % win > t_r + win - WINDOW)
    r_n = lax.broadcasted_iota(jnp.int32, (rows, steps * bb), 0)
    c_n = lax.broadcasted_iota(jnp.int32, (rows, steps * bb), 1)
    t_rn, b_rn = (r_n % (steps * bb)) // bb, r_n % bb
    dt_n = t_rn - c_n // bb
    mask_n = (c_n % bb == b_rn) & (dt_n >= 0) & (dt_n < WINDOW)
    g_row = lax.broadcasted_iota(jnp.int32, (rows, 1), 0) // (steps * bb)
    scale = HEAD_DIM ** -0.5
    nt = (((1,), (1,)), ((), ()))
    q = q_ref[...].reshape(steps * bb, n_heads * HEAD_DIM)
    kn = kn_ref[...].reshape(steps * bb, N_KV_HEADS * HEAD_DIM)
    vn = vn_ref[...].reshape(steps * bb, N_KV_HEADS * HEAD_DIM)
    outs = []
    for kv in range(N_KV_HEADS):
        ls = slice(kv * HEAD_DIM, (kv + 1) * HEAD_DIM)
        qs = jnp.concatenate([q[:, (kv * group + gi) * HEAD_DIM:(kv * group + gi + 1) * HEAD_DIM]
                              for gi in range(group)], axis=0).astype(_BF16)
        sink = jnp.zeros((rows, 1), _F32)
        for gi in range(group):
            sink = jnp.where(g_row == gi, sink_ref[kv * group + gi], sink)
        s_c = lax.dot_general(qs, kc_ref[:, ls].astype(_BF16), nt, preferred_element_type=_F32) * scale
        s_n = lax.dot_general(qs, kn[:, ls].astype(_BF16), nt, preferred_element_type=_F32) * scale
        s_c = jnp.where(mask_c, s_c, NEG_INF)
        s_n = jnp.where(mask_n, s_n, NEG_INF)
        p_c, p_n = _softmax_sink([s_c, s_n], sink)
        o = (jnp.dot(p_c.astype(_BF16), vc_ref[:, ls].astype(_BF16), preferred_element_type=_F32)
             + jnp.dot(p_n.astype(_BF16), vn[:, ls].astype(_BF16), preferred_element_type=_F32))
        outs.extend(o[gi * steps * bb:(gi + 1) * steps * bb, :] for gi in range(group))
    o_ref[...] = jnp.concatenate(outs, axis=1).reshape(o_ref.shape).astype(o_ref.dtype)


def _attn_sample(q3, kn3, vn3, kc2, vc2, sinks, *, bb):
    steps, nb, aw = q3.shape
    kvw = kn3.shape[2]
    win = kc2.shape[0] // nb
    tok = lambda i: (0, i, 0)
    return pl.pallas_call(
        functools.partial(_attn_sample_kernel, n_heads=aw // HEAD_DIM),
        grid=(nb // bb,),
        in_specs=[pl.BlockSpec(memory_space=pltpu.SMEM),
                  pl.BlockSpec((steps, bb, aw), tok), pl.BlockSpec((steps, bb, kvw), tok),
                  pl.BlockSpec((steps, bb, kvw), tok),
                  pl.BlockSpec((bb * win, kvw), lambda i: (i, 0)),
                  pl.BlockSpec((bb * win, kvw), lambda i: (i, 0))],
        out_specs=pl.BlockSpec((steps, bb, aw), tok),
        out_shape=jax.ShapeDtypeStruct((steps, nb, aw), _BF16),
        compiler_params=_params(),
        name="attn_sample",
    )(sinks, q3, kn3, vn3, kc2, vc2)


def _ssm_sample_kernel(u_ref, s0r_ref, s0i_ref, ab_ref, bd_ref, cr_ref, ci_ref, d_ref,
                       g_ref, sr_ref, si_ref, *, steps):
    nb = s0r_ref.shape[0]
    ns = s0r_ref.shape[1]
    u = u_ref[...]
    bu = jnp.dot(u.astype(_BF16), bd_ref[...], preferred_element_type=_F32)
    a_re, a_im = ab_ref[0:1, :], ab_ref[1:2, :]
    xr, xi = s0r_ref[...], s0i_ref[...]
    for t in range(steps):
        rs = slice(t * nb, (t + 1) * nb)
        nr = a_re * xr - a_im * xi + bu[rs, :ns]
        ni = a_re * xi + a_im * xr + bu[rs, ns:]
        xr, xi = nr, ni
        y = (jnp.dot(xr.astype(_BF16), cr_ref[...], preferred_element_type=_F32)
             - jnp.dot(xi.astype(_BF16), ci_ref[...], preferred_element_type=_F32))
        g_ref[rs, :] = jax.nn.gelu(y + d_ref[...] * u[rs, :])
    sr_ref[...] = xr
    si_ref[...] = xi


def _ssm_sample(u2, s0r, s0i, ab2, bd_b, bd_cr, bd_ci, d_skip, *, steps):
    n, sw = u2.shape
    nb, ns = s0r.shape
    return pl.pallas_call(
        functools.partial(_ssm_sample_kernel, steps=steps),
        grid=(1,),
        in_specs=[_const_spec(u2.shape), _const_spec(s0r.shape), _const_spec(s0i.shape),
                  _const_spec(ab2.shape), _const_spec(bd_b.shape), _const_spec(bd_cr.shape),
                  _const_spec(bd_ci.shape), _const_spec((1, sw))],
        out_specs=[_const_spec((n, sw)), _const_spec((nb, ns)), _const_spec((nb, ns))],
        out_shape=[jax.ShapeDtypeStruct((n, sw), _F32), jax.ShapeDtypeStruct((nb, ns), _F32),
                   jax.ShapeDtypeStruct((nb, ns), _F32)],
        compiler_params=_params(),
        name="ssm_sample",
    )(u2, s0r, s0i, ab2, bd_b, bd_cr, bd_ci, d_skip.reshape(1, sw))


def _block_diag(blocks):
    g, r, c = blocks.shape
    eye = jnp.eye(g, dtype=blocks.dtype)
    return (blocks[:, :, None, :] * eye[:, None, :, None]).reshape(g * r, g * c)


def _tile_rows(n, target):
    tm = min(n, target)
    while n % tm:
        tm //= 2
    return tm


def _prompt_layer(x, w, prep):
    b, l, d = x.shape
    n = b * l
    x2 = x.reshape(n, d)
    tm = _tile_rows(l, 512)
    tables = _rotary_tables(jnp.arange(l, dtype=jnp.int32))
    q, k, v, u_t = _inproj(x2, w["g_mix_pre"], w["wqkv"], w["wu_t"], tables, tm=tm,
                           u_transposed=True, q_dtype=_BF16, u_dtype=_BF16)
    aw, kvw = q.shape[1], k.shape[1]
    att = _attn_prompt(q.reshape(b, l, aw), k.reshape(b, l, kvw), v.reshape(b, l, kvw), w["sinks"])
    sw = u_t.shape[0]
    kc = l // CHUNK
    m, bs, ca, _, apow, _ = prep
    g_t3, states = _ssm_prompt(u_t.reshape(sw, n // CHUNK, CHUNK), w["d_skip"], m, bs, ca, apow, kc)
    y = _mix_mlp(x2, att.reshape(n, aw), g_t3.reshape(sw, n), w["wglu_t"], w["b_glu"], w["wout"],
                 w["g_mix_post"], w["g_mlp_pre"], w["wup"], w["wdown"], w["g_mlp_post"],
                 tm=_tile_rows(n, 256), g_transposed=True)
    keep = min(WINDOW, l)
    k_state = k.reshape(b, l, N_KV_HEADS, HEAD_DIM)[:, l - keep:]
    v_state = v.reshape(b, l, N_KV_HEADS, HEAD_DIM)[:, l - keep:]
    n_grp = m.shape[0]
    fin = states[:, kc - 1::kc, :]
    s_re = jnp.swapaxes(fin[:, :, :SSM_STATE], 0, 1)
    s_im = jnp.swapaxes(fin[:, :, SSM_STATE:], 0, 1)
    del n_grp
    return y.reshape(b, l, d), k_state, v_state, s_re, s_im


def _sample_layer(x, cache_k, cache_v, s0_re, s0_im, w, prep):
    nb, steps, d = x.shape
    n = nb * steps
    xs = jnp.swapaxes(x, 0, 1).reshape(n, d)
    pos = PAST_LEN + jnp.arange(steps, dtype=jnp.int32)
    tables = tuple(jnp.repeat(t, nb, axis=0) for t in _rotary_tables(pos))
    q, k, v, u = _inproj(xs, w["g_mix_pre"], w["wqkv"], w["wu"], tables, tm=n,
                         u_transposed=False, q_dtype=_F32, u_dtype=_F32)
    aw, kvw = q.shape[1], k.shape[1]
    win = cache_k.shape[1]
    att = _attn_sample(q.reshape(steps, nb, aw), k.reshape(steps, nb, kvw), v.reshape(steps, nb, kvw),
                       cache_k.reshape(nb * win, kvw), cache_v.reshape(nb * win, kvw), w["sinks"],
                       bb=_tile_rows(nb, 8))
    _, _, _, ab, _, bbt = prep
    p = SSM_STATE
    n_grp = ab.shape[0]
    ab2 = jnp.stack([ab[:, 0, :p].reshape(-1), ab[:, 0, p:].reshape(-1)])
    bd_b = jnp.concatenate([_block_diag(bbt[:, :, :p]), _block_diag(bbt[:, :, p:])], axis=1).astype(_BF16)
    bd_cr = _block_diag(jnp.swapaxes(w["c_re"], 1, 2)).astype(_BF16)
    bd_ci = _block_diag(jnp.swapaxes(w["c_im"], 1, 2)).astype(_BF16)
    gact, s_re, s_im = _ssm_sample(u, s0_re.reshape(nb, n_grp * p), s0_im.reshape(nb, n_grp * p),
                                   ab2, bd_b, bd_cr, bd_ci, w["d_skip"], steps=steps)
    y = _mix_mlp(xs, att.reshape(n, aw), gact, w["wglu"], w["b_glu"], w["wout"],
                 w["g_mix_post"], w["g_mlp_pre"], w["wup"], w["wdown"], w["g_mlp_post"],
                 tm=_tile_rows(n, 256), g_transposed=False)
    y = jnp.swapaxes(y.reshape(steps, nb, d), 0, 1)
    k_new = jnp.swapaxes(k.reshape(steps, nb, N_KV_HEADS, HEAD_DIM), 0, 1)
    v_new = jnp.swapaxes(v.reshape(steps, nb, N_KV_HEADS, HEAD_DIM), 0, 1)
    k_state = jnp.concatenate([cache_k, k_new], axis=1)[:, -win:]
    v_state = jnp.concatenate([cache_v, v_new], axis=1)[:, -win:]
    return y, k_state, v_state, s_re.reshape(nb, n_grp, p), s_im.reshape(nb, n_grp, p)


def kernel(x_prompt, x_sample, cache_k, cache_v, state_ssm_re, state_ssm_im, g_mix_pre, w_in, sinks,
           a_re, a_im, log_dt, b_re, b_im, c_re, c_im, d_skip, w_glu, b_glu, w_out, g_mix_post,
           g_mlp_pre, w_up, w_down, g_mlp_post):
    depth = w_in.shape[0]
    seq = x_prompt.shape[1]
    assert seq % CHUNK == 0 and seq % WINDOW == 0
    chunks_per_seq = seq // CHUNK
    n_pow = max(1, (chunks_per_seq - 1).bit_length())
    ssm_width = d_skip.shape[1]
    qkv_width = w_in.shape[2] - ssm_width
    yp, ys = x_prompt, x_sample
    outs = [[] for _ in range(8)]
    for l in range(depth):
        w = {
            "g_mix_pre": g_mix_pre[l], "sinks": sinks[l], "d_skip": d_skip[l], "b_glu": b_glu[l],
            "g_mix_post": g_mix_post[l], "g_mlp_pre": g_mlp_pre[l], "g_mlp_post": g_mlp_post[l],
            "c_re": c_re[l], "c_im": c_im[l],
            "wqkv": w_in[l, :, :qkv_width].astype(_BF16),
            "wu": w_in[l, :, qkv_width:].astype(_BF16),
            "wu_t": w_in[l, :, qkv_width:].T.astype(_BF16),
            "wglu": w_glu[l].astype(_BF16), "wglu_t": w_glu[l].T.astype(_BF16),
            "wout": w_out[l].astype(_BF16), "wup": w_up[l].astype(_BF16), "wdown": w_down[l].astype(_BF16),
        }
        prep = _ssm_prep(a_re[l], a_im[l], log_dt[l], b_re[l], b_im[l], c_re[l], c_im[l], n_pow)
        yp, kp, vp, srp, sip = _prompt_layer(yp, w, prep)
        ys, kss, vss, srs, sis = _sample_layer(ys, cache_k[l], cache_v[l], state_ssm_re[l],
                                               state_ssm_im[l], w, prep)
        for lst, val in zip(outs, (kp, vp, srp, sip, kss, vss, srs, sis)):
            lst.append(val)
    return (yp, ys) + tuple(jnp.stack(o) for o in outs)
```

```python
import functools
import math

import jax
import jax.numpy as jnp
from jax import lax
from jax.experimental import pallas as pl
from jax.experimental.pallas import tpu as pltpu

HEAD_DIM = 64
N_KV_HEADS = 2
WINDOW = 128
ROT_DIM = HEAD_DIM // 4
ROPE_THETA = 500000.0
SSM_GROUP = 16
SSM_STATE = 64
NORM_EPS = 1e-6
NEG_INF = -1e30
PAST_LEN = 16384
CHUNK = 128
LANES = 128
VMEM_LIMIT = 56 * 1024 * 1024

_F32 = jnp.float32
_BF16 = jnp.bfloat16


def _params(n_parallel=1):
    return pltpu.CompilerParams(dimension_semantics=("arbitrary",) * n_parallel,
                                vmem_limit_bytes=VMEM_LIMIT)


def _rms(x, g):
    return x * lax.rsqrt(jnp.mean(x * x, axis=-1, keepdims=True) + NORM_EPS) * g


def _const_spec(shape):
    nd = len(shape)
    return pl.BlockSpec(shape, lambda *_: (0,) * nd, pipeline_mode=pl.Buffered(1))


def _rotary_tables(pos):
    half = ROT_DIM // 2
    inv_freq = jnp.power(ROPE_THETA, -jnp.arange(half, dtype=_F32) * (2.0 / ROT_DIM))
    ang = pos.astype(_F32)[:, None] * inv_freq[None, :]
    cos, sin = jnp.cos(ang), jnp.sin(ang)
    n = pos.shape[0]
    ones = jnp.ones((n, HEAD_DIM - ROT_DIM), _F32)
    zeros_h = jnp.zeros((n, half), _F32)
    zeros_r = jnp.zeros((n, HEAD_DIM - ROT_DIM), _F32)
    c_tab = jnp.concatenate([cos, cos, ones], axis=1)
    s_up = jnp.concatenate([-sin, zeros_h, zeros_r], axis=1)
    s_dn = jnp.concatenate([zeros_h, sin, zeros_r], axis=1)
    rep = LANES // HEAD_DIM
    return tuple(jnp.tile(t, (1, rep)) for t in (c_tab, s_up, s_dn))


def _inproj_kernel(x_ref, g_ref, wqkv_ref, wu_ref, c_ref, su_ref, sd_ref,
                   q_ref, k_ref, v_ref, u_ref, *, attn_width, kv_width, u_transposed):
    half = ROT_DIM // 2
    h = _rms(x_ref[...], g_ref[...]).astype(_BF16)
    qkv = jnp.dot(h, wqkv_ref[...], preferred_element_type=_F32)
    c_tab, s_up, s_dn = c_ref[...], su_ref[...], sd_ref[...]

    def rot(blk):
        return (blk * c_tab + pltpu.roll(blk, LANES - half, axis=1) * s_up
                + pltpu.roll(blk, half, axis=1) * s_dn)

    for j in range(attn_width // LANES):
        q_ref[:, j * LANES:(j + 1) * LANES] = rot(qkv[:, j * LANES:(j + 1) * LANES]).astype(q_ref.dtype)
    for j in range(kv_width // LANES):
        o = attn_width + j * LANES
        k_ref[:, j * LANES:(j + 1) * LANES] = rot(qkv[:, o:o + LANES])
    v_ref[...] = qkv[:, attn_width + kv_width:]
    if u_transposed:
        u = lax.dot_general(wu_ref[...], h, (((1,), (1,)), ((), ())), preferred_element_type=_F32)
    else:
        u = jnp.dot(h, wu_ref[...], preferred_element_type=_F32)
    u_ref[...] = u.astype(u_ref.dtype)


def _inproj(x2, g, wqkv, wu, tables, *, tm, u_transposed, q_dtype, u_dtype):
    n, d = x2.shape
    kv_width = N_KV_HEADS * HEAD_DIM
    attn_width = wqkv.shape[1] - 2 * kv_width
    ssm_width = wu.shape[0] if u_transposed else wu.shape[1]
    nt = n // tm
    tab_blocks = tables[0].shape[0] // tm
    tab_spec = pl.BlockSpec((tm, LANES), lambda i: (i % tab_blocks, 0))
    if u_transposed:
        u_shape, u_spec = (ssm_width, n), pl.BlockSpec((ssm_width, tm), lambda i: (0, i))
    else:
        u_shape, u_spec = (n, ssm_width), pl.BlockSpec((tm, ssm_width), lambda i: (i, 0))
    return pl.pallas_call(
        functools.partial(_inproj_kernel, attn_width=attn_width, kv_width=kv_width,
                          u_transposed=u_transposed),
        grid=(nt,),
        in_specs=[pl.BlockSpec((tm, d), lambda i: (i, 0)), _const_spec((1, d)),
                  _const_spec(wqkv.shape), _const_spec(wu.shape), tab_spec, tab_spec, tab_spec],
        out_specs=[pl.BlockSpec((tm, attn_width), lambda i: (i, 0)),
                   pl.BlockSpec((tm, kv_width), lambda i: (i, 0)),
                   pl.BlockSpec((tm, kv_width), lambda i: (i, 0)), u_spec],
        out_shape=[jax.ShapeDtypeStruct((n, attn_width), q_dtype),
                   jax.ShapeDtypeStruct((n, kv_width), _F32),
                   jax.ShapeDtypeStruct((n, kv_width), _F32),
                   jax.ShapeDtypeStruct(u_shape, u_dtype)],
        compiler_params=_params(),
        name="inproj_t" if u_transposed else "inproj_n",
    )(x2, g.reshape(1, d), wqkv, wu, *tables)


def _softmax_sink(s_parts, sink):
    m = sink
    for s in s_parts:
        m = jnp.maximum(m, jnp.max(s, axis=-1, keepdims=True))
    p_parts = [jnp.exp(s - m) for s in s_parts]
    denom = jnp.exp(sink - m)
    for p in p_parts:
        denom = denom + jnp.sum(p, axis=-1, keepdims=True)
    return [p / denom for p in p_parts]


def _attn_prompt_kernel(sink_ref, q_ref, kp_ref, kc_ref, vp_ref, vc_ref, o_ref, *, n_heads):
    blk = q_ref.shape[0]
    i_blk = pl.program_id(1)
    heads_per_lane_block = LANES // HEAD_DIM
    kv_lane_blocks = n_heads // N_KV_HEADS // heads_per_lane_block
    row = lax.broadcasted_iota(jnp.int32, (blk, blk), 0)
    col = lax.broadcasted_iota(jnp.int32, (blk, blk), 1)
    own = col <= row
    valid = own | (i_blk > 0)
    scale = HEAD_DIM ** -0.5
    nt = (((1,), (1,)), ((), ()))
    vv = jnp.concatenate([vp_ref[...], vc_ref[...]], axis=0).astype(_BF16)
    pad = jnp.zeros((2 * blk, HEAD_DIM), _BF16)
    p_rows, inv_denoms = [], []
    for kv in range(N_KV_HEADS):
        ls = slice(kv * HEAD_DIM, (kv + 1) * HEAD_DIM)
        kk = jnp.concatenate([kp_ref[:, ls], kc_ref[:, ls]], axis=0).astype(_BF16)
        kk2 = jnp.concatenate([jnp.concatenate([kk, pad], axis=1),
                               jnp.concatenate([pad, kk], axis=1)], axis=0)
        for pr in range(kv_lane_blocks):
            j = kv * kv_lane_blocks + pr
            qp = q_ref[:, j * LANES:(j + 1) * LANES] * scale
            s2 = lax.dot_general(qp, kk2, nt, preferred_element_type=_F32)
            for e in range(heads_per_lane_block):
                sink = sink_ref[j * heads_per_lane_block + e]
                s_prev = s2[:, (2 * e) * blk:(2 * e + 1) * blk]
                s_own = s2[:, (2 * e + 1) * blk:(2 * e + 2) * blk]
                s = jnp.where(valid, jnp.where(own, s_own, s_prev), NEG_INF)
                m = jnp.maximum(jnp.max(s, axis=-1, keepdims=True), sink)
                p = jnp.exp(s - m)
                denom = jnp.sum(p, axis=-1, keepdims=True) + jnp.exp(sink - m)
                p_rows.append(jnp.concatenate([jnp.where(own, 0.0, p), jnp.where(own, p, 0.0)],
                                              axis=1).astype(_BF16))
                inv_denoms.append(1.0 / denom)
    o_all = jnp.dot(jnp.concatenate(p_rows, axis=0), vv, preferred_element_type=_F32)
    lane = lax.broadcasted_iota(jnp.int32, (blk, LANES), 1)
    for j in range(n_heads // heads_per_lane_block):
        kv = j // kv_lane_blocks
        parts = []
        for e in range(heads_per_lane_block):
            hd = j * heads_per_lane_block + e
            o_h = o_all[hd * blk:(hd + 1) * blk, :] * inv_denoms[hd]
            shift = (e - kv) * HEAD_DIM % LANES
            parts.append(pltpu.roll(o_h, shift, axis=1) if shift else o_h)
        out = parts[0]
        for e in range(1, heads_per_lane_block):
            out = jnp.where(lane >= e * HEAD_DIM, parts[e], out)
        o_ref[:, j * LANES:(j + 1) * LANES] = out.astype(o_ref.dtype)


def _attn_prompt(q3, k3, v3, sinks):
    b, l, aw = q3.shape
    kvw = k3.shape[2]
    nb = l // WINDOW
    cur = lambda bi, i: (bi, i, 0)
    prev = lambda bi, i: (bi, jnp.maximum(i - 1, 0), 0)
    return pl.pallas_call(
        functools.partial(_attn_prompt_kernel, n_heads=aw // HEAD_DIM),
        grid=(b, nb),
        in_specs=[pl.BlockSpec(memory_space=pltpu.SMEM),
                  pl.BlockSpec((None, WINDOW, aw), cur),
                  pl.BlockSpec((None, WINDOW, kvw), prev), pl.BlockSpec((None, WINDOW, kvw), cur),
                  pl.BlockSpec((None, WINDOW, kvw), prev), pl.BlockSpec((None, WINDOW, kvw), cur)],
        out_specs=pl.BlockSpec((None, WINDOW, aw), cur),
        out_shape=jax.ShapeDtypeStruct((b, l, aw), _BF16),
        compiler_params=_params(2),
        name="attn_prompt",
    )(sinks, q3, k3, k3, v3, v3)


def _cmul(ar, ai, br, bi):
    return ar * br - ai * bi, ar * bi + ai * br


def _discretise(ar, ai, dt):
    mag = jnp.exp(ar * dt)
    ab_re, ab_im = mag * jnp.cos(ai * dt), mag * jnp.sin(ai * dt)
    den = ar * ar + ai * ai
    nr, ni = ab_re - 1.0, ab_im
    f_re = (nr * ar + ni * ai) / den
    f_im = (ni * ar - nr * ai) / den
    return ab_re, ab_im, f_re, f_im


def _pack(re, im):
    return jnp.concatenate([re, im], axis=1)


def _cmul_packed(a, s):
    p = a.shape[-1] // 2
    a_rr = jnp.concatenate([a[:, :p], a[:, :p]], axis=1)
    a_ii = jnp.concatenate([-a[:, p:], a[:, p:]], axis=1)
    return s * a_rr + pltpu.roll(s, p, axis=1) * a_ii


def _cpow_packed(base, tau, n_bits):
    p = base.shape[-1] // 2
    lane = lax.broadcasted_iota(jnp.int32, base.shape, 1)
    one = jnp.where(lane < p, 1.0, 0.0)
    out = None
    for bit in range(n_bits):
        factor = jnp.where(((tau >> bit) & 1) == 1, base, one)
        out = factor if out is None else _cmul_packed(factor, out)
        if bit + 1 < n_bits:
            base = _cmul_packed(base, base)
    return out


def _ssm_prep_kernel(ldt_ref, a_re_ref, a_im_ref, bt_re_ref, bt_im_ref, c_re_ref, c_im_ref,
                     m_ref, bs_ref, cat_ref, ab_ref, apow_ref, bbt_ref, kt_ref):
    t_len, p_len, c_len = CHUNK, SSM_STATE, SSM_GROUP
    dt = jnp.exp(ldt_ref[...])
    ab_re, ab_im, f_re, f_im = _discretise(a_re_ref[...], a_im_ref[...], dt)
    ab = _pack(ab_re, ab_im)
    ab_ref[...] = ab
    bb_re, bb_im = _cmul(f_re, f_im, bt_re_ref[...], bt_im_ref[...])
    bb = _pack(bb_re, bb_im)
    bbt_ref[...] = bb

    tau = lax.broadcasted_iota(jnp.int32, (t_len, 2 * p_len), 0)
    n_bits = (t_len - 1).bit_length()
    w_fwd = _cpow_packed(ab, tau, n_bits)
    w_rev = _cpow_packed(ab, t_len - 1 - tau, n_bits)
    w_next = _cmul_packed(ab, w_fwd)
    step = ab
    for _ in range(n_bits):
        step = _cmul_packed(step, step)
    rows = []
    for _ in range(apow_ref.shape[0]):
        rows.append(step)
        step = _cmul_packed(step, step)
    apow_ref[...] = jnp.concatenate(rows, axis=0)

    c_re, c_im = c_re_ref[...], c_im_ref[...]
    g_rows = []
    for cp in range(c_len):
        g_re, g_im = _cmul(bb_re[cp:cp + 1, :], bb_im[cp:cp + 1, :], c_re, c_im)
        g_rows.append(_pack(g_re, -g_im))
    g2 = jnp.concatenate(g_rows, axis=0)
    kt_ref[...] = lax.dot_general(g2, w_fwd, (((1,), (1,)), ((), ())), preferred_element_type=_F32,
                                  precision=lax.Precision.HIGHEST)

    s_idx = lax.broadcasted_iota(jnp.int32, (t_len, t_len), 0)
    t_idx = lax.broadcasted_iota(jnp.int32, (t_len, t_len), 1)
    causal = t_idx >= s_idx

    def toeplitz_rows(cp, carry):
        for c in range(c_len):
            taps = kt_ref[pl.ds(cp * c_len + c, 1), :]
            full = jnp.broadcast_to(taps, (t_len, t_len))
            skew = pltpu.roll(full, 0, axis=1, stride=1, stride_axis=0)
            m_ref[pl.ds(pl.multiple_of(cp * t_len, t_len), t_len), c * t_len:(c + 1) * t_len] = (
                jnp.where(causal, skew, 0.0).astype(m_ref.dtype))
        return carry

    lax.fori_loop(0, c_len, toeplitz_rows, 0)

    for cp in range(c_len):
        bs_ref[cp * t_len:(cp + 1) * t_len, :] = _cmul_packed(bb[cp:cp + 1, :], w_rev).astype(bs_ref.dtype)

    lane = lax.broadcasted_iota(jnp.int32, (1, 2 * p_len), 1)
    conj = jnp.where(lane < p_len, 1.0, -1.0)
    c_pk = _pack(c_re, c_im)
    for c in range(c_len):
        cat_ref[c * t_len:(c + 1) * t_len, :] = (_cmul_packed(c_pk[c:c + 1, :], w_next) * conj).astype(cat_ref.dtype)


def _ssm_prep(a_re, a_im, log_dt, b_re, b_im, c_re, c_im, n_pow):
    g, p = a_re.shape
    c = b_re.shape[2]
    tw = c * CHUNK
    grp = lambda *shape: pl.BlockSpec((None,) + shape, lambda i: (i,) + (0,) * len(shape))
    return pl.pallas_call(
        _ssm_prep_kernel,
        grid=(g,),
        in_specs=[grp(1, 1), grp(1, p), grp(1, p), grp(c, p), grp(c, p), grp(c, p), grp(c, p)],
        out_specs=[grp(tw, tw), grp(tw, 2 * p), grp(tw, 2 * p), grp(1, 2 * p), grp(n_pow, 2 * p),
                   grp(c, 2 * p)],
        out_shape=[jax.ShapeDtypeStruct((g, tw, tw), _BF16),
                   jax.ShapeDtypeStruct((g, tw, 2 * p), _BF16),
                   jax.ShapeDtypeStruct((g, tw, 2 * p), _BF16),
                   jax.ShapeDtypeStruct((g, 1, 2 * p), _F32),
                   jax.ShapeDtypeStruct((g, n_pow, 2 * p), _F32),
                   jax.ShapeDtypeStruct((g, c, 2 * p), _F32)],
        scratch_shapes=[pltpu.VMEM((c * c, CHUNK), _F32)],
        compiler_params=_params(),
        name="ssm_prep",
    )(log_dt.reshape(g, 1, 1), a_re.reshape(g, 1, p), a_im.reshape(g, 1, p),
      jnp.swapaxes(b_re, 1, 2), jnp.swapaxes(b_im, 1, 2), c_re, c_im)


def _ssm_prompt_kernel(d_ref, u_ref, m_ref, bs_ref, cat_ref, apow_ref, g_ref, st_ref, *, chunks_per_seq):
    c_len, n_tok = u_ref.shape
    nc = n_tok // CHUNK
    grp = pl.program_id(0)
    u3 = u_ref[...].reshape(c_len, nc, CHUNK)
    x = jnp.concatenate([u3[c] for c in range(c_len)], axis=1)
    y = jnp.dot(x, m_ref[...], preferred_element_type=_F32)
    s = jnp.dot(x, bs_ref[...], preferred_element_type=_F32)
    k_idx = lax.broadcasted_iota(jnp.int32, s.shape, 0) % chunks_per_seq
    for j in range(apow_ref.shape[0]):
        d = 1 << j
        shifted = jnp.where(k_idx >= d, pltpu.roll(s, d, axis=0), 0.0)
        s = s + _cmul_packed(apow_ref[j:j + 1, :], shifted)
    st_ref[...] = s
    s_prev = jnp.where(k_idx >= 1, pltpu.roll(s, 1, axis=0), 0.0)
    y = y + lax.dot_general(s_prev.astype(_BF16), cat_ref[...], (((1,), (1,)), ((), ())),
                            preferred_element_type=_F32)
    g_slabs = []
    for c in range(c_len):
        yc = y[:, c * CHUNK:(c + 1) * CHUNK] + d_ref[grp * c_len + c] * u3[c].astype(_F32)
        g_slabs.append(jax.nn.gelu(yc).astype(g_ref.dtype))
    g_ref[...] = jnp.stack(g_slabs, axis=0).reshape(c_len, n_tok)


def _ssm_prompt(u_t, d_skip, m, bs, cat, apow, chunks_per_seq):
    width, n = u_t.shape
    g, tw, _ = m.shape
    c = width // g
    nc = n // CHUNK
    p2 = bs.shape[2]
    n_pow = apow.shape[1]
    grp = lambda *shape: pl.BlockSpec((None,) + shape, lambda i: (i,) + (0,) * len(shape))
    return pl.pallas_call(
        functools.partial(_ssm_prompt_kernel, chunks_per_seq=chunks_per_seq),
        grid=(g,),
        in_specs=[pl.BlockSpec(memory_space=pltpu.SMEM),
                  pl.BlockSpec((c, n), lambda i: (i, 0)),
                  grp(tw, tw), grp(tw, p2), grp(tw, p2), grp(n_pow, p2)],
        out_specs=[pl.BlockSpec((c, n), lambda i: (i, 0)), grp(nc, p2)],
        out_shape=[jax.ShapeDtypeStruct((width, n), _BF16),
                   jax.ShapeDtypeStruct((g, nc, p2), _F32)],
        compiler_params=_params(),
        name="ssm_prompt",
    )(d_skip, u_t, m, bs, cat, apow)


def _mix_mlp_kernel(x_ref, att_ref, g_ref, wglu_ref, bglu_ref, wout_ref, gpost_ref, gpre_ref,
                    wup_ref, wdown_ref, gmlp_ref, o_ref, *, g_transposed, ff_chunk):
    aw = att_ref.shape[1]
    gact = g_ref[...]
    if g_transposed:
        z = jnp.dot(wglu_ref[...], gact, preferred_element_type=_F32) + bglu_ref[...]
    else:
        z = jnp.dot(gact.astype(_BF16), wglu_ref[...], preferred_element_type=_F32) + bglu_ref[...]
    ssm = (gact.astype(_F32) * (1.0 / (1.0 + jnp.exp(-z)))).astype(_BF16)
    mix = jnp.dot(att_ref[...], wout_ref[:aw, :], preferred_element_type=_F32)
    if g_transposed:
        mix = mix + lax.dot_general(ssm, wout_ref[aw:, :], (((0,), (0,)), ((), ())),
                                    preferred_element_type=_F32)
    else:
        mix = mix + jnp.dot(ssm, wout_ref[aw:, :], preferred_element_type=_F32)
    x1 = x_ref[...] + _rms(mix, gpost_ref[...])
    h = _rms(x1, gpre_ref[...]).astype(_BF16)
    d_ff = wup_ref.shape[1]
    f = None
    for j in range(d_ff // ff_chunk):
        up = jnp.dot(h, wup_ref[:, j * ff_chunk:(j + 1) * ff_chunk], preferred_element_type=_F32)
        act = jnp.square(jnp.maximum(up, 0.0)).astype(_BF16)
        part = jnp.dot(act, wdown_ref[j * ff_chunk:(j + 1) * ff_chunk, :], preferred_element_type=_F32)
        f = part if f is None else f + part
    o_ref[...] = x1 + _rms(f, gmlp_ref[...])


def _mix_mlp(x2, att, gact, wglu, bglu, wout, gpost, gpre, wup, wdown, gmlp, *, tm, g_transposed):
    n, d = x2.shape
    aw = att.shape[1]
    sw = wglu.shape[0]
    d_ff = wup.shape[1]
    if g_transposed:
        g_spec = pl.BlockSpec((sw, tm), lambda i: (0, i))
        bglu2 = bglu.reshape(sw, 1)
    else:
        g_spec = pl.BlockSpec((tm, sw), lambda i: (i, 0))
        bglu2 = bglu.reshape(1, sw)
    return pl.pallas_call(
        functools.partial(_mix_mlp_kernel, g_transposed=g_transposed, ff_chunk=min(d_ff, 1024)),
        grid=(n // tm,),
        in_specs=[pl.BlockSpec((tm, d), lambda i: (i, 0)), pl.BlockSpec((tm, aw), lambda i: (i, 0)), g_spec,
                  _const_spec(wglu.shape), _const_spec(bglu2.shape), _const_spec(wout.shape),
                  _const_spec((1, d)), _const_spec((1, d)), _const_spec(wup.shape),
                  _const_spec(wdown.shape), _const_spec((1, d))],
        out_specs=pl.BlockSpec((tm, d), lambda i: (i, 0)),
        out_shape=jax.ShapeDtypeStruct((n, d), _F32),
        compiler_params=_params(),
        name="mix_mlp_t" if g_transposed else "mix_mlp_n",
    )(x2, att, gact, wglu, bglu2, wout, gpost.reshape(1, d), gpre.reshape(1, d), wup, wdown,
      gmlp.reshape(1, d))


def _attn_sample_kernel(sink_ref, q_ref, kn_ref, vn_ref, kc_ref, vc_ref, o_ref, *, n_heads):
    steps, bb, _ = q_ref.shape
    win = kc_ref.shape[0] // bb
    group = n_heads // N_KV_HEADS
    rows = group * steps * bb
    r_c = lax.broadcasted_iota(jnp.int32, (rows, bb * win), 0)
    c_c = lax.broadcasted_iota(jnp.int32, (rows, bb * win), 1)
    t_r, b_r = (r_c % (steps * bb)) // bb, r_c % bb
    mask_c = (c_c // win == b_r) & (c_c % win > t_r + win - WINDOW)
    r_n = lax.broadcasted_iota(jnp.int32, (rows, steps * bb), 0)
    c_n = lax.broadcasted_iota(jnp.int32, (rows, steps * bb), 1)
    t_rn, b_rn = (r_n % (steps * bb)) // bb, r_n % bb
    dt_n = t_rn - c_n // bb
    mask_n = (c_n % bb == b_rn) & (dt_n >= 0) & (dt_n < WINDOW)
    g_row = lax.broadcasted_iota(jnp.int32, (rows, 1), 0) // (steps * bb)
    scale = HEAD_DIM ** -0.5
    nt = (((1,), (1,)), ((), ()))
    q = q_ref[...].reshape(steps * bb, n_heads * HEAD_DIM)
    kn = kn_ref[...].reshape(steps * bb, N_KV_HEADS * HEAD_DIM)
    vn = vn_ref[...].reshape(steps * bb, N_KV_HEADS * HEAD_DIM)
    outs = []
    for kv in range(N_KV_HEADS):
        ls = slice(kv * HEAD_DIM, (kv + 1) * HEAD_DIM)
        qs = jnp.concatenate([q[:, (kv * group + gi) * HEAD_DIM:(kv * group + gi + 1) * HEAD_DIM]
                              for gi in range(group)], axis=0).astype(_BF16)
        sink = jnp.zeros((rows, 1), _F32)
        for gi in range(group):
            sink = jnp.where(g_row == gi, sink_ref[kv * group + gi], sink)
        s_c = lax.dot_general(qs, kc_ref[:, ls].astype(_BF16), nt, preferred_element_type=_F32) * scale
        s_n = lax.dot_general(qs, kn[:, ls].astype(_BF16), nt, preferred_element_type=_F32) * scale
        s_c = jnp.where(mask_c, s_c, NEG_INF)
        s_n = jnp.where(mask_n, s_n, NEG_INF)
        p_c, p_n = _softmax_sink([s_c, s_n], sink)
        o = (jnp.dot(p_c.astype(_BF16), vc_ref[:, ls].astype(_BF16), preferred_element_type=_F32)
             + jnp.dot(p_n.astype(_BF16), vn[:, ls].astype(_BF16), preferred_element_type=_F32))
        outs.extend(o[gi * steps * bb:(gi + 1) * steps * bb, :] for gi in range(group))
    o_ref[...] = jnp.concatenate(outs, axis=1).reshape(o_ref.shape).astype(o_ref.dtype)


def _attn_sample(q3, kn3, vn3, kc2, vc2, sinks, *, bb):
    steps, nb, aw = q3.shape
    kvw = kn3.shape[2]
    win = kc2.shape[0] // nb
    tok = lambda i: (0, i, 0)
    return pl.pallas_call(
        functools.partial(_attn_sample_kernel, n_heads=aw // HEAD_DIM),
        grid=(nb // bb,),
        in_specs=[pl.BlockSpec(memory_space=pltpu.SMEM),
                  pl.BlockSpec((steps, bb, aw), tok), pl.BlockSpec((steps, bb, kvw), tok),
                  pl.BlockSpec((steps, bb, kvw), tok),
                  pl.BlockSpec((bb * win, kvw), lambda i: (i, 0)),
                  pl.BlockSpec((bb * win, kvw), lambda i: (i, 0))],
        out_specs=pl.BlockSpec((steps, bb, aw), tok),
        out_shape=jax.ShapeDtypeStruct((steps, nb, aw), _BF16),
        compiler_params=_params(),
        name="attn_sample",
    )(sinks, q3, kn3, vn3, kc2, vc2)


def _ssm_sample_kernel(u_ref, s0r_ref, s0i_ref, ab_ref, bd_ref, cr_ref, ci_ref, d_ref,
                       g_ref, sr_ref, si_ref, *, steps):
    nb = s0r_ref.shape[0]
    ns = s0r_ref.shape[1]
    u = u_ref[...]
    bu = jnp.dot(u.astype(_BF16), bd_ref[...], preferred_element_type=_F32)
    a_re, a_im = ab_ref[0:1, :], ab_ref[1:2, :]
    xr, xi = s0r_ref[...], s0i_ref[...]
    for t in range(steps):
        rs = slice(t * nb, (t + 1) * nb)
        nr = a_re * xr - a_im * xi + bu[rs, :ns]
        ni = a_re * xi + a_im * xr + bu[rs, ns:]
        xr, xi = nr, ni
        y = (jnp.dot(xr.astype(_BF16), cr_ref[...], preferred_element_type=_F32)
             - jnp.dot(xi.astype(_BF16), ci_ref[...], preferred_element_type=_F32))
        g_ref[rs, :] = jax.nn.gelu(y + d_ref[...] * u[rs, :])
    sr_ref[...] = xr
    si_ref[...] = xi


def _ssm_sample(u2, s0r, s0i, ab2, bd_b, bd_cr, bd_ci, d_skip, *, steps):
    n, sw = u2.shape
    nb, ns = s0r.shape
    return pl.pallas_call(
        functools.partial(_ssm_sample_kernel, steps=steps),
        grid=(1,),
        in_specs=[_const_spec(u2.shape), _const_spec(s0r.shape), _const_spec(s0i.shape),
                  _const_spec(ab2.shape), _const_spec(bd_b.shape), _const_spec(bd_cr.shape),
                  _const_spec(bd_ci.shape), _const_spec((1, sw))],
        out_specs=[pl.BlockSpec((n, sw), lambda i: (0, 0)), pl.BlockSpec((nb, ns), lambda i: (0, 0)),
                   pl.BlockSpec((nb, ns), lambda i: (0, 0))],
        out_shape=[jax.ShapeDtypeStruct((n, sw), _F32), jax.ShapeDtypeStruct((nb, ns), _F32),
                   jax.ShapeDtypeStruct((nb, ns), _F32)],
        compiler_params=_params(),
        name="ssm_sample",
    )(u2, s0r, s0i, ab2, bd_b, bd_cr, bd_ci, d_skip.reshape(1, sw))


def _block_diag(blocks):
    g, r, c = blocks.shape
    eye = jnp.eye(g, dtype=blocks.dtype)
    return (blocks[:, :, None, :] * eye[:, None, :, None]).reshape(g * r, g * c)


def _tile_rows(n, target):
    tm = min(n, target)
    while n % tm:
        tm //= 2
    return tm


def _prompt_layer(x, w, prep):
    b, l, d = x.shape
    n = b * l
    x2 = x.reshape(n, d)
    tm = _tile_rows(l, 512)
    tables = _rotary_tables(jnp.arange(l, dtype=jnp.int32))
    q, k, v, u_t = _inproj(x2, w["g_mix_pre"], w["wqkv"], w["wu_t"], tables, tm=tm,
                           u_transposed=True, q_dtype=_BF16, u_dtype=_BF16)
    aw, kvw = q.shape[1], k.shape[1]
    att = _attn_prompt(q.reshape(b, l, aw), k.reshape(b, l, kvw), v.reshape(b, l, kvw), w["sinks"])
    sw = u_t.shape[0]
    kc = l // CHUNK
    m, bs, ca, _, apow, _ = prep
    g_t, states = _ssm_prompt(u_t, w["d_skip"], m, bs, ca, apow, kc)
    y = _mix_mlp(x2, att.reshape(n, aw), g_t, w["wglu_t"], w["b_glu"], w["wout"],
                 w["g_mix_post"], w["g_mlp_pre"], w["wup"], w["wdown"], w["g_mlp_post"],
                 tm=_tile_rows(n, 512), g_transposed=True)
    keep = min(WINDOW, l)
    k_state = k.reshape(b, l, N_KV_HEADS, HEAD_DIM)[:, l - keep:]
    v_state = v.reshape(b, l, N_KV_HEADS, HEAD_DIM)[:, l - keep:]
    n_grp = m.shape[0]
    fin = states[:, kc - 1::kc, :]
    s_re = jnp.swapaxes(fin[:, :, :SSM_STATE], 0, 1)
    s_im = jnp.swapaxes(fin[:, :, SSM_STATE:], 0, 1)
    del n_grp
    return y.reshape(b, l, d), k_state, v_state, s_re, s_im


def _sample_layer(x, cache_k, cache_v, s0_re, s0_im, w, prep):
    nb, steps, d = x.shape
    n = nb * steps
    xs = jnp.swapaxes(x, 0, 1).reshape(n, d)
    pos = PAST_LEN + jnp.arange(steps, dtype=jnp.int32)
    tables = tuple(jnp.repeat(t, nb, axis=0) for t in _rotary_tables(pos))
    q, k, v, u = _inproj(xs, w["g_mix_pre"], w["wqkv"], w["wu"], tables, tm=n,
                         u_transposed=False, q_dtype=_F32, u_dtype=_F32)
    aw, kvw = q.shape[1], k.shape[1]
    win = cache_k.shape[1]
    att = _attn_sample(q.reshape(steps, nb, aw), k.reshape(steps, nb, kvw), v.reshape(steps, nb, kvw),
                       cache_k.reshape(nb * win, kvw), cache_v.reshape(nb * win, kvw), w["sinks"],
                       bb=_tile_rows(nb, 8))
    _, _, _, ab, _, bbt = prep
    p = SSM_STATE
    n_grp = ab.shape[0]
    ab2 = jnp.stack([ab[:, 0, :p].reshape(-1), ab[:, 0, p:].reshape(-1)])
    bd_b = jnp.concatenate([_block_diag(bbt[:, :, :p]), _block_diag(bbt[:, :, p:])], axis=1).astype(_BF16)
    bd_cr = _block_diag(jnp.swapaxes(w["c_re"], 1, 2)).astype(_BF16)
    bd_ci = _block_diag(jnp.swapaxes(w["c_im"], 1, 2)).astype(_BF16)
    gact, s_re, s_im = _ssm_sample(u, s0_re.reshape(nb, n_grp * p), s0_im.reshape(nb, n_grp * p),
                                   ab2, bd_b, bd_cr, bd_ci, w["d_skip"], steps=steps)
    y = _mix_mlp(xs, att.reshape(n, aw), gact, w["wglu"], w["b_glu"], w["wout"],
                 w["g_mix_post"], w["g_mlp_pre"], w["wup"], w["wdown"], w["g_mlp_post"],
                 tm=_tile_rows(n, 256), g_transposed=False)
    y = jnp.swapaxes(y.reshape(steps, nb, d), 0, 1)
    k_new = jnp.swapaxes(k.reshape(steps, nb, N_KV_HEADS, HEAD_DIM), 0, 1)
    v_new = jnp.swapaxes(v.reshape(steps, nb, N_KV_HEADS, HEAD_DIM), 0, 1)
    k_state = jnp.concatenate([cache_k, k_new], axis=1)[:, -win:]
    v_state = jnp.concatenate([cache_v, v_new], axis=1)[:, -win:]
    return y, k_state, v_state, s_re.reshape(nb, n_grp, p), s_im.reshape(nb, n_grp, p)


def kernel(x_prompt, x_sample, cache_k, cache_v, state_ssm_re, state_ssm_im, g_mix_pre, w_in, sinks,
           a_re, a_im, log_dt, b_re, b_im, c_re, c_im, d_skip, w_glu, b_glu, w_out, g_mix_post,
           g_mlp_pre, w_up, w_down, g_mlp_post):
    depth = w_in.shape[0]
    seq = x_prompt.shape[1]
    assert seq % CHUNK == 0 and seq % WINDOW == 0
    chunks_per_seq = seq // CHUNK
    n_pow = max(1, (chunks_per_seq - 1).bit_length())
    ssm_width = d_skip.shape[1]
    qkv_width = w_in.shape[2] - ssm_width
    yp, ys = x_prompt, x_sample
    outs = [[] for _ in range(8)]
    for l in range(depth):
        w = {
            "g_mix_pre": g_mix_pre[l], "sinks": sinks[l], "d_skip": d_skip[l], "b_glu": b_glu[l],
            "g_mix_post": g_mix_post[l], "g_mlp_pre": g_mlp_pre[l], "g_mlp_post": g_mlp_post[l],
            "c_re": c_re[l], "c_im": c_im[l],
            "wqkv": w_in[l, :, :qkv_width].astype(_BF16),
            "wu": w_in[l, :, qkv_width:].astype(_BF16),
            "wu_t": w_in[l, :, qkv_width:].T.astype(_BF16),
            "wglu": w_glu[l].astype(_BF16), "wglu_t": w_glu[l].T.astype(_BF16),
            "wout": w_out[l].astype(_BF16), "wup": w_up[l].astype(_BF16), "wdown": w_down[l].astype(_BF16),
        }
        prep = _ssm_prep(a_re[l], a_im[l], log_dt[l], b_re[l], b_im[l], c_re[l], c_im[l], n_pow)
        yp, kp, vp, srp, sip = _prompt_layer(yp, w, prep)
        ys, kss, vss, srs, sis = _sample_layer(ys, cache_k[l], cache_v[l], state_ssm_re[l],
                                               state_ssm_im[l], w, prep)
        for lst, val in zip(outs, (kp, vp, srp, sip, kss, vss, srs, sis)):
            lst.append(val)
    return (yp, ys) + tuple(jnp.stack(o) for o in outs)
```

```python
import functools
import math

import jax
import jax.numpy as jnp
from jax import lax
from jax.experimental import pallas as pl
from jax.experimental.pallas import tpu as pltpu

HEAD_DIM = 64
N_KV_HEADS = 2
WINDOW = 128
ROT_DIM = HEAD_DIM // 4
ROPE_THETA = 500000.0
SSM_GROUP = 16
SSM_STATE = 64
NORM_EPS = 1e-6
NEG_INF = -1e30
PAST_LEN = 16384
CHUNK = 128
LANES = 128
VMEM_LIMIT = 56 * 1024 * 1024

_F32 = jnp.float32
_BF16 = jnp.bfloat16


def _params(n_parallel=1):
    return pltpu.CompilerParams(dimension_semantics=("arbitrary",) * n_parallel,
                                vmem_limit_bytes=VMEM_LIMIT)


def _rms(x, g):
    return x * lax.rsqrt(jnp.mean(x * x, axis=-1, keepdims=True) + NORM_EPS) * g


def _const_spec(shape):
    nd = len(shape)
    return pl.BlockSpec(shape, lambda *_: (0,) * nd, pipeline_mode=pl.Buffered(1))


def _rotary_tables(pos):
    half = ROT_DIM // 2
    inv_freq = jnp.power(ROPE_THETA, -jnp.arange(half, dtype=_F32) * (2.0 / ROT_DIM))
    ang = pos.astype(_F32)[:, None] * inv_freq[None, :]
    cos, sin = jnp.cos(ang), jnp.sin(ang)
    n = pos.shape[0]
    ones = jnp.ones((n, HEAD_DIM - ROT_DIM), _F32)
    zeros_h = jnp.zeros((n, half), _F32)
    zeros_r = jnp.zeros((n, HEAD_DIM - ROT_DIM), _F32)
    c_tab = jnp.concatenate([cos, cos, ones], axis=1)
    s_up = jnp.concatenate([-sin, zeros_h, zeros_r], axis=1)
    s_dn = jnp.concatenate([zeros_h, sin, zeros_r], axis=1)
    rep = LANES // HEAD_DIM
    return tuple(jnp.tile(t, (1, rep)) for t in (c_tab, s_up, s_dn))


def _inproj_kernel(x_ref, g_ref, wqkv_ref, wu_ref, c_ref, su_ref, sd_ref,
                   q_ref, k_ref, v_ref, u_ref, *, attn_width, kv_width, u_transposed):
    half = ROT_DIM // 2
    h = _rms(x_ref[...], g_ref[...]).astype(_BF16)
    qkv = jnp.dot(h, wqkv_ref[...], preferred_element_type=_F32)
    c_tab, s_up, s_dn = c_ref[...], su_ref[...], sd_ref[...]

    def rot(blk):
        return (blk * c_tab + pltpu.roll(blk, LANES - half, axis=1) * s_up
                + pltpu.roll(blk, half, axis=1) * s_dn)

    for j in range(attn_width // LANES):
        q_ref[:, j * LANES:(j + 1) * LANES] = rot(qkv[:, j * LANES:(j + 1) * LANES]).astype(q_ref.dtype)
    for j in range(kv_width // LANES):
        o = attn_width + j * LANES
        k_ref[:, j * LANES:(j + 1) * LANES] = rot(qkv[:, o:o + LANES])
    v_ref[...] = qkv[:, attn_width + kv_width:]
    if u_transposed:
        u = lax.dot_general(wu_ref[...], h, (((1,), (1,)), ((), ())), preferred_element_type=_F32)
    else:
        u = jnp.dot(h, wu_ref[...], preferred_element_type=_F32)
    u_ref[...] = u.astype(u_ref.dtype)


def _inproj(x2, g, wqkv, wu, tables, *, tm, u_transposed, q_dtype, u_dtype):
    n, d = x2.shape
    kv_width = N_KV_HEADS * HEAD_DIM
    attn_width = wqkv.shape[1] - 2 * kv_width
    ssm_width = wu.shape[0] if u_transposed else wu.shape[1]
    nt = n // tm
    tab_blocks = tables[0].shape[0] // tm
    tab_spec = pl.BlockSpec((tm, LANES), lambda i: (i % tab_blocks, 0))
    if u_transposed:
        u_shape, u_spec = (ssm_width, n), pl.BlockSpec((ssm_width, tm), lambda i: (0, i))
    else:
        u_shape, u_spec = (n, ssm_width), pl.BlockSpec((tm, ssm_width), lambda i: (i, 0))
    return pl.pallas_call(
        functools.partial(_inproj_kernel, attn_width=attn_width, kv_width=kv_width,
                          u_transposed=u_transposed),
        grid=(nt,),
        in_specs=[pl.BlockSpec((tm, d), lambda i: (i, 0)), _const_spec((1, d)),
                  _const_spec(wqkv.shape), _const_spec(wu.shape), tab_spec, tab_spec, tab_spec],
        out_specs=[pl.BlockSpec((tm, attn_width), lambda i: (i, 0)),
                   pl.BlockSpec((tm, kv_width), lambda i: (i, 0)),
                   pl.BlockSpec((tm, kv_width), lambda i: (i, 0)), u_spec],
        out_shape=[jax.ShapeDtypeStruct((n, attn_width), q_dtype),
                   jax.ShapeDtypeStruct((n, kv_width), _F32),
                   jax.ShapeDtypeStruct((n, kv_width), _F32),
                   jax.ShapeDtypeStruct(u_shape, u_dtype)],
        compiler_params=_params(),
        name="inproj_t" if u_transposed else "inproj_n",
    )(x2, g.reshape(1, d), wqkv, wu, *tables)


def _softmax_sink(s_parts, sink):
    m = sink
    for s in s_parts:
        m = jnp.maximum(m, jnp.max(s, axis=-1, keepdims=True))
    p_parts = [jnp.exp(s - m) for s in s_parts]
    denom = jnp.exp(sink - m)
    for p in p_parts:
        denom = denom + jnp.sum(p, axis=-1, keepdims=True)
    return [p / denom for p in p_parts]


def _attn_prompt_kernel(sink_ref, q_ref, kp_ref, kc_ref, vp_ref, vc_ref, o_ref, *, n_heads):
    blk = kp_ref.shape[0]
    n_sub = q_ref.shape[0] // blk
    first_step = pl.program_id(1) == 0
    heads_per_lane_block = LANES // HEAD_DIM
    kv_lane_blocks = n_heads // N_KV_HEADS // heads_per_lane_block
    row = lax.broadcasted_iota(jnp.int32, (blk, blk), 0)
    col = lax.broadcasted_iota(jnp.int32, (blk, blk), 1)
    own = col <= row
    lane = lax.broadcasted_iota(jnp.int32, (1, LANES), 1)
    scale = HEAD_DIM ** -0.5
    nt = (((1,), (1,)), ((), ()))
    k_all = jnp.concatenate([kp_ref[...], kc_ref[...]], axis=0)
    v_all = jnp.concatenate([vp_ref[...], vc_ref[...]], axis=0).astype(_BF16)
    k_swap = pltpu.roll(k_all, HEAD_DIM, axis=1)
    k_lo = [jnp.where(lane < HEAD_DIM, k_all if kv == 0 else k_swap, 0.0).astype(_BF16) for kv in range(N_KV_HEADS)]
    k_hi = [jnp.where(lane >= HEAD_DIM, k_swap if kv == 0 else k_all, 0.0).astype(_BF16) for kv in range(N_KV_HEADS)]
    for sub in range(n_sub):
        rows = slice(sub * blk, (sub + 1) * blk)
        keys = slice(sub * blk, (sub + 2) * blk)
        valid = (own | jnp.logical_not(first_step)) if sub == 0 else None
        p_rows, inv_denoms = [], []
        for kv in range(N_KV_HEADS):
            kk2 = jnp.concatenate([k_lo[kv][keys], k_hi[kv][keys]], axis=0)
            for pr in range(kv_lane_blocks):
                j = kv * kv_lane_blocks + pr
                qp = q_ref[rows, j * LANES:(j + 1) * LANES] * scale
                s2 = lax.dot_general(qp, kk2, nt, preferred_element_type=_F32)
                for e in range(heads_per_lane_block):
                    sink = sink_ref[j * heads_per_lane_block + e]
                    s_prev = s2[:, (2 * e) * blk:(2 * e + 1) * blk]
                    s_own = s2[:, (2 * e + 1) * blk:(2 * e + 2) * blk]
                    s = jnp.where(own, s_own, s_prev)
                    if valid is not None:
                        s = jnp.where(valid, s, NEG_INF)
                    m = jnp.maximum(jnp.max(s, axis=-1, keepdims=True), sink)
                    p = jnp.exp(s - m)
                    denom = jnp.sum(p, axis=-1, keepdims=True) + jnp.exp(sink - m)
                    p_rows.append(jnp.concatenate([jnp.where(own, 0.0, p), jnp.where(own, p, 0.0)],
                                                  axis=1).astype(_BF16))
                    inv_denoms.append(1.0 / denom)
        o_all = jnp.dot(jnp.concatenate(p_rows, axis=0), v_all[keys], preferred_element_type=_F32)
        for j in range(n_heads // heads_per_lane_block):
            kv = j // kv_lane_blocks
            parts = []
            for e in range(heads_per_lane_block):
                hd = j * heads_per_lane_block + e
                o_h = o_all[hd * blk:(hd + 1) * blk, :] * inv_denoms[hd]
                shift = (e - kv) * HEAD_DIM % LANES
                parts.append(pltpu.roll(o_h, shift, axis=1) if shift else o_h)
            out = parts[0]
            for e in range(1, heads_per_lane_block):
                out = jnp.where(lane >= e * HEAD_DIM, parts[e], out)
            o_ref[rows, j * LANES:(j + 1) * LANES] = out.astype(o_ref.dtype)


def _attn_prompt(q3, k3, v3, sinks, *, blocks_per_step):
    b, l, aw = q3.shape
    kvw = k3.shape[2]
    tq = blocks_per_step * WINDOW
    cur = lambda bi, i: (bi, i, 0)
    prev = lambda bi, i: (bi, jnp.maximum(i * blocks_per_step - 1, 0), 0)
    return pl.pallas_call(
        functools.partial(_attn_prompt_kernel, n_heads=aw // HEAD_DIM),
        grid=(b, l // tq),
        in_specs=[pl.BlockSpec(memory_space=pltpu.SMEM),
                  pl.BlockSpec((None, tq, aw), cur),
                  pl.BlockSpec((None, WINDOW, kvw), prev), pl.BlockSpec((None, tq, kvw), cur),
                  pl.BlockSpec((None, WINDOW, kvw), prev), pl.BlockSpec((None, tq, kvw), cur)],
        out_specs=pl.BlockSpec((None, tq, aw), cur),
        out_shape=jax.ShapeDtypeStruct((b, l, aw), _BF16),
        compiler_params=_params(2),
        name="attn_prompt",
    )(sinks, q3, k3, k3, v3, v3)


def _cmul(ar, ai, br, bi):
    return ar * br - ai * bi, ar * bi + ai * br


def _discretise(ar, ai, dt):
    mag = jnp.exp(ar * dt)
    ab_re, ab_im = mag * jnp.cos(ai * dt), mag * jnp.sin(ai * dt)
    den = ar * ar + ai * ai
    nr, ni = ab_re - 1.0, ab_im
    f_re = (nr * ar + ni * ai) / den
    f_im = (ni * ar - nr * ai) / den
    return ab_re, ab_im, f_re, f_im


def _pack(re, im):
    return jnp.concatenate([re, im], axis=1)


def _cmul_packed(a, s, s_swapped=None):
    p = a.shape[-1] // 2
    if s_swapped is None:
        s_swapped = pltpu.roll(s, p, axis=1)
    a_rr = jnp.concatenate([a[:, :p], a[:, :p]], axis=1)
    a_ii = jnp.concatenate([-a[:, p:], a[:, p:]], axis=1)
    return s * a_rr + s_swapped * a_ii


def _cpow_bits(base_re, base_im, tau, n_bits):
    out_re = out_im = None
    for bit in range(n_bits):
        on = ((tau >> bit) & 1) == 1
        f_re, f_im = jnp.where(on, base_re, 1.0), jnp.where(on, base_im, 0.0)
        out_re, out_im = (f_re, f_im) if out_re is None else _cmul(out_re, out_im, f_re, f_im)
        if bit + 1 < n_bits:
            base_re, base_im = _cmul(base_re, base_im, base_re, base_im)
    return out_re, out_im


def _ssm_prep_kernel(ldt_ref, a_re_ref, a_im_ref, bt_re_ref, bt_im_ref, c_re_ref, c_im_ref,
                     m_ref, bs_ref, cat_ref, ab_ref, apow_ref, bbt_ref, kt_ref):
    t_len, p_len, c_len = CHUNK, SSM_STATE, SSM_GROUP
    dt = jnp.exp(ldt_ref[...])
    ab_re, ab_im, f_re, f_im = _discretise(a_re_ref[...], a_im_ref[...], dt)
    ab = _pack(ab_re, ab_im)
    ab_ref[...] = ab
    bb_re, bb_im = _cmul(f_re, f_im, bt_re_ref[...], bt_im_ref[...])
    bb = _pack(bb_re, bb_im)
    bbt_ref[...] = bb

    row = lax.broadcasted_iota(jnp.int32, (t_len, 2 * p_len), 0)
    lo = lax.broadcasted_iota(jnp.int32, (t_len, 2 * p_len), 1) < p_len
    n_bits = (t_len - 1).bit_length()
    pw_re, pw_im = _cpow_bits(_pack(ab_re, ab_re), _pack(ab_im, ab_im),
                              jnp.where(lo, row, t_len - 1 - row), n_bits)
    pw_re_sw, pw_im_sw = pltpu.roll(pw_re, p_len, axis=1), pltpu.roll(pw_im, p_len, axis=1)
    w_fwd, w_fwd_sw = jnp.where(lo, pw_re, pw_im_sw), jnp.where(lo, pw_im, pw_re_sw)
    w_rev, w_rev_sw = jnp.where(lo, pw_re_sw, pw_im), jnp.where(lo, pw_im_sw, pw_re)
    w_next = _cmul_packed(ab, w_fwd, w_fwd_sw)
    w_next_sw = _cmul_packed(ab, w_fwd_sw, -w_fwd)
    st_re, st_im = ab_re, ab_im
    for _ in range(n_bits):
        st_re, st_im = _cmul(st_re, st_im, st_re, st_im)
    rows = []
    for _ in range(apow_ref.shape[0]):
        rows.append(_pack(st_re, st_im))
        st_re, st_im = _cmul(st_re, st_im, st_re, st_im)
    apow_ref[...] = jnp.concatenate(rows, axis=0)

    c_re, c_im = c_re_ref[...], c_im_ref[...]
    g_rows = []
    for cp in range(c_len):
        g_re, g_im = _cmul(bb_re[cp:cp + 1, :], bb_im[cp:cp + 1, :], c_re, c_im)
        g_rows.append(_pack(g_re, -g_im))
    g2 = jnp.concatenate(g_rows, axis=0)
    kt_ref[...] = lax.dot_general(g2, w_fwd, (((1,), (1,)), ((), ())), preferred_element_type=_F32,
                                  precision=lax.Precision.HIGHEST)

    s_idx = lax.broadcasted_iota(jnp.int32, (t_len, t_len), 0)
    t_idx = lax.broadcasted_iota(jnp.int32, (t_len, t_len), 1)
    causal = t_idx >= s_idx

    def toeplitz_rows(cp, carry):
        for c in range(c_len):
            taps = kt_ref[pl.ds(cp * c_len + c, 1), :]
            full = jnp.broadcast_to(taps, (t_len, t_len))
            skew = pltpu.roll(full, 0, axis=1, stride=1, stride_axis=0)
            m_ref[pl.ds(pl.multiple_of(cp * t_len, t_len), t_len), c * t_len:(c + 1) * t_len] = (
                jnp.where(causal, skew, 0.0).astype(m_ref.dtype))
        return carry

    lax.fori_loop(0, c_len, toeplitz_rows, 0, unroll=2)

    for cp in range(c_len):
        bs_ref[cp * t_len:(cp + 1) * t_len, :] = _cmul_packed(bb[cp:cp + 1, :], w_rev, w_rev_sw).astype(bs_ref.dtype)

    lane = lax.broadcasted_iota(jnp.int32, (1, 2 * p_len), 1)
    conj = jnp.where(lane < p_len, 1.0, -1.0)
    c_pk = _pack(c_re, c_im)
    for c in range(c_len):
        cat_ref[c * t_len:(c + 1) * t_len, :] = (
            _cmul_packed(c_pk[c:c + 1, :], w_next, w_next_sw) * conj).astype(cat_ref.dtype)


def _ssm_prep(a_re, a_im, log_dt, b_re, b_im, c_re, c_im, n_pow):
    g, p = a_re.shape
    c = b_re.shape[2]
    tw = c * CHUNK
    grp = lambda *shape: pl.BlockSpec((None,) + shape, lambda i: (i,) + (0,) * len(shape))
    return pl.pallas_call(
        _ssm_prep_kernel,
        grid=(g,),
        in_specs=[grp(1, 1), grp(1, p), grp(1, p), grp(c, p), grp(c, p), grp(c, p), grp(c, p)],
        out_specs=[grp(tw, tw), grp(tw, 2 * p), grp(tw, 2 * p), grp(1, 2 * p), grp(n_pow, 2 * p),
                   grp(c, 2 * p)],
        out_shape=[jax.ShapeDtypeStruct((g, tw, tw), _BF16),
                   jax.ShapeDtypeStruct((g, tw, 2 * p), _BF16),
                   jax.ShapeDtypeStruct((g, tw, 2 * p), _BF16),
                   jax.ShapeDtypeStruct((g, 1, 2 * p), _F32),
                   jax.ShapeDtypeStruct((g, n_pow, 2 * p), _F32),
                   jax.ShapeDtypeStruct((g, c, 2 * p), _F32)],
        scratch_shapes=[pltpu.VMEM((c * c, CHUNK), _F32)],
        compiler_params=_params(),
        name="ssm_prep",
    )(log_dt.reshape(g, 1, 1), a_re.reshape(g, 1, p), a_im.reshape(g, 1, p),
      jnp.swapaxes(b_re, 1, 2), jnp.swapaxes(b_im, 1, 2), c_re, c_im)


def _ssm_prompt_kernel(d_ref, u_ref, m_ref, bs_ref, cat_ref, apow_ref, g_ref, st_ref, *, chunks_per_seq):
    c_len, n_tok = u_ref.shape
    nc = n_tok // CHUNK
    grp = pl.program_id(0)
    u3 = u_ref[...].reshape(c_len, nc, CHUNK)
    x = jnp.concatenate([u3[c] for c in range(c_len)], axis=1)
    y = jnp.dot(x, m_ref[...], preferred_element_type=_F32)
    s = jnp.dot(x, bs_ref[...], preferred_element_type=_F32)
    k_idx = lax.broadcasted_iota(jnp.int32, s.shape, 0) % chunks_per_seq
    for j in range(apow_ref.shape[0]):
        d = 1 << j
        shifted = jnp.where(k_idx >= d, pltpu.roll(s, d, axis=0), 0.0)
        s = s + _cmul_packed(apow_ref[j:j + 1, :], shifted)
    st_ref[...] = s
    s_prev = jnp.where(k_idx >= 1, pltpu.roll(s, 1, axis=0), 0.0)
    y = y + lax.dot_general(s_prev.astype(_BF16), cat_ref[...], (((1,), (1,)), ((), ())),
                            preferred_element_type=_F32)
    g_slabs = []
    for c in range(c_len):
        yc = y[:, c * CHUNK:(c + 1) * CHUNK] + d_ref[grp * c_len + c] * u3[c].astype(_F32)
        g_slabs.append(jax.nn.gelu(yc).astype(g_ref.dtype))
    g_ref[...] = jnp.stack(g_slabs, axis=0).reshape(c_len, n_tok)


def _ssm_prompt(u_t, d_skip, m, bs, cat, apow, chunks_per_seq):
    width, n = u_t.shape
    g, tw, _ = m.shape
    c = width // g
    nc = n // CHUNK
    p2 = bs.shape[2]
    n_pow = apow.shape[1]
    grp = lambda *shape: pl.BlockSpec((None,) + shape, lambda i: (i,) + (0,) * len(shape))
    return pl.pallas_call(
        functools.partial(_ssm_prompt_kernel, chunks_per_seq=chunks_per_seq),
        grid=(g,),
        in_specs=[pl.BlockSpec(memory_space=pltpu.SMEM),
                  pl.BlockSpec((c, n), lambda i: (i, 0)),
                  grp(tw, tw), grp(tw, p2), grp(tw, p2), grp(n_pow, p2)],
        out_specs=[pl.BlockSpec((c, n), lambda i: (i, 0)), grp(nc, p2)],
        out_shape=[jax.ShapeDtypeStruct((width, n), _BF16),
                   jax.ShapeDtypeStruct((g, nc, p2), _F32)],
        compiler_params=_params(),
        name="ssm_prompt",
    )(d_skip, u_t, m, bs, cat, apow)


def _mix_mlp_kernel(x_ref, att_ref, g_ref, wglu_ref, bglu_ref, wout_ref, gpost_ref, gpre_ref,
                    wup_ref, wdown_ref, gmlp_ref, o_ref, *, g_transposed, ff_chunk):
    aw = att_ref.shape[1]
    gact = g_ref[...]
    if g_transposed:
        z = jnp.dot(wglu_ref[...], gact, preferred_element_type=_F32) + bglu_ref[...]
    else:
        z = jnp.dot(gact.astype(_BF16), wglu_ref[...], preferred_element_type=_F32) + bglu_ref[...]
    ssm = (gact.astype(_F32) * (1.0 / (1.0 + jnp.exp(-z)))).astype(_BF16)
    mix = jnp.dot(att_ref[...], wout_ref[:aw, :], preferred_element_type=_F32)
    if g_transposed:
        mix = mix + lax.dot_general(ssm, wout_ref[aw:, :], (((0,), (0,)), ((), ())),
                                    preferred_element_type=_F32)
    else:
        mix = mix + jnp.dot(ssm, wout_ref[aw:, :], preferred_element_type=_F32)
    x1 = x_ref[...] + _rms(mix, gpost_ref[...])
    h = _rms(x1, gpre_ref[...]).astype(_BF16)
    d_ff = wup_ref.shape[1]
    f = None
    for j in range(d_ff // ff_chunk):
        up = jnp.dot(h, wup_ref[:, j * ff_chunk:(j + 1) * ff_chunk], preferred_element_type=_F32)
        act = jnp.square(jnp.maximum(up, 0.0)).astype(_BF16)
        part = jnp.dot(act, wdown_ref[j * ff_chunk:(j + 1) * ff_chunk, :], preferred_element_type=_F32)
        f = part if f is None else f + part
    o_ref[...] = x1 + _rms(f, gmlp_ref[...])


def _mix_mlp(x2, att, gact, wglu, bglu, wout, gpost, gpre, wup, wdown, gmlp, *, tm, g_transposed):
    n, d = x2.shape
    aw = att.shape[1]
    sw = wglu.shape[0]
    d_ff = wup.shape[1]
    if g_transposed:
        g_spec = pl.BlockSpec((sw, tm), lambda i: (0, i))
        bglu2 = bglu.reshape(sw, 1)
    else:
        g_spec = pl.BlockSpec((tm, sw), lambda i: (i, 0))
        bglu2 = bglu.reshape(1, sw)
    return pl.pallas_call(
        functools.partial(_mix_mlp_kernel, g_transposed=g_transposed, ff_chunk=min(d_ff, 1024)),
        grid=(n // tm,),
        in_specs=[pl.BlockSpec((tm, d), lambda i: (i, 0)), pl.BlockSpec((tm, aw), lambda i: (i, 0)), g_spec,
                  _const_spec(wglu.shape), _const_spec(bglu2.shape), _const_spec(wout.shape),
                  _const_spec((1, d)), _const_spec((1, d)), _const_spec(wup.shape),
                  _const_spec(wdown.shape), _const_spec((1, d))],
        out_specs=pl.BlockSpec((tm, d), lambda i: (i, 0)),
        out_shape=jax.ShapeDtypeStruct((n, d), _F32),
        compiler_params=_params(),
        name="mix_mlp_t" if g_transposed else "mix_mlp_n",
    )(x2, att, gact, wglu, bglu2, wout, gpost.reshape(1, d), gpre.reshape(1, d), wup, wdown,
      gmlp.reshape(1, d))


def _attn_sample_kernel(sink_ref, q_ref, kn_ref, vn_ref, kc_ref, vc_ref, o_ref, *, n_heads):
    steps, bb, _ = q_ref.shape
    win = kc_ref.shape[3]
    group = n_heads // N_KV_HEADS
    rows = group * steps * bb
    r_c = lax.broadcasted_iota(jnp.int32, (rows, bb * win), 0)
    c_c = lax.broadcasted_iota(jnp.int32, (rows, bb * win), 1)
    t_r, b_r = (r_c % (steps * bb)) // bb, r_c % bb
    mask_c = (c_c // win == b_r) & (c_c % win > t_r + win - WINDOW)
    r_n = lax.broadcasted_iota(jnp.int32, (rows, steps * bb), 0)
    c_n = lax.broadcasted_iota(jnp.int32, (rows, steps * bb), 1)
    t_rn, b_rn = (r_n % (steps * bb)) // bb, r_n % bb
    dt_n = t_rn - c_n // bb
    mask_n = (c_n % bb == b_rn) & (dt_n >= 0) & (dt_n < WINDOW)
    g_row = lax.broadcasted_iota(jnp.int32, (rows, 1), 0) // (steps * bb)
    scale = HEAD_DIM ** -0.5
    nt = (((1,), (1,)), ((), ()))
    q = q_ref[...].reshape(steps * bb, n_heads * HEAD_DIM)
    kn = kn_ref[...].reshape(steps * bb, N_KV_HEADS * HEAD_DIM)
    vn = vn_ref[...].reshape(steps * bb, N_KV_HEADS * HEAD_DIM)
    outs = []
    for kv in range(N_KV_HEADS):
        ls = slice(kv * HEAD_DIM, (kv + 1) * HEAD_DIM)
        qs = jnp.concatenate([q[:, (kv * group + gi) * HEAD_DIM:(kv * group + gi + 1) * HEAD_DIM]
                              for gi in range(group)], axis=0).astype(_BF16)
        sink = jnp.zeros((rows, 1), _F32)
        for gi in range(group):
            sink = jnp.where(g_row == gi, sink_ref[kv * group + gi], sink)
        kc_t = jnp.concatenate([kc_ref[b, kv] for b in range(bb)], axis=1).astype(_BF16)
        vc_t = jnp.concatenate([vc_ref[b, kv] for b in range(bb)], axis=1).astype(_BF16)
        s_c = jnp.dot(qs, kc_t, preferred_element_type=_F32) * scale
        s_n = lax.dot_general(qs, kn[:, ls].astype(_BF16), nt, preferred_element_type=_F32) * scale
        s_c = jnp.where(mask_c, s_c, NEG_INF)
        s_n = jnp.where(mask_n, s_n, NEG_INF)
        p_c, p_n = _softmax_sink([s_c, s_n], sink)
        o = (lax.dot_general(p_c.astype(_BF16), vc_t, nt, preferred_element_type=_F32)
             + jnp.dot(p_n.astype(_BF16), vn[:, ls].astype(_BF16), preferred_element_type=_F32))
        outs.extend(o[gi * steps * bb:(gi + 1) * steps * bb, :] for gi in range(group))
    o_ref[...] = jnp.concatenate(outs, axis=1).reshape(o_ref.shape).astype(o_ref.dtype)


def _attn_sample(q3, kn3, vn3, kc_t, vc_t, sinks, *, bb):
    steps, nb, aw = q3.shape
    kvw = kn3.shape[2]
    tok = lambda i: (0, i, 0)
    cache_spec = pl.BlockSpec((bb,) + kc_t.shape[1:], lambda i: (i, 0, 0, 0))
    return pl.pallas_call(
        functools.partial(_attn_sample_kernel, n_heads=aw // HEAD_DIM),
        grid=(nb // bb,),
        in_specs=[pl.BlockSpec(memory_space=pltpu.SMEM),
                  pl.BlockSpec((steps, bb, aw), tok), pl.BlockSpec((steps, bb, kvw), tok),
                  pl.BlockSpec((steps, bb, kvw), tok), cache_spec, cache_spec],
        out_specs=pl.BlockSpec((steps, bb, aw), tok),
        out_shape=jax.ShapeDtypeStruct((steps, nb, aw), _BF16),
        compiler_params=_params(),
        name="attn_sample",
    )(sinks, q3, kn3, vn3, kc_t, vc_t)


def _ssm_sample_kernel(u_ref, s0r_ref, s0i_ref, ab_ref, bd_ref, cr_ref, ci_ref, d_ref,
                       g_ref, sr_ref, si_ref, *, steps):
    nb = s0r_ref.shape[0]
    ns = s0r_ref.shape[1]
    u = u_ref[...]
    bu = jnp.dot(u.astype(_BF16), bd_ref[...], preferred_element_type=_F32)
    a_re, a_im = ab_ref[0:1, :], ab_ref[1:2, :]
    xr, xi = s0r_ref[...], s0i_ref[...]
    for t in range(steps):
        rs = slice(t * nb, (t + 1) * nb)
        nr = a_re * xr - a_im * xi + bu[rs, :ns]
        ni = a_re * xi + a_im * xr + bu[rs, ns:]
        xr, xi = nr, ni
        y = (jnp.dot(xr.astype(_BF16), cr_ref[...], preferred_element_type=_F32)
             - jnp.dot(xi.astype(_BF16), ci_ref[...], preferred_element_type=_F32))
        g_ref[rs, :] = jax.nn.gelu(y + d_ref[...] * u[rs, :])
    sr_ref[...] = xr
    si_ref[...] = xi


def _ssm_sample(u2, s0r, s0i, ab2, bd_b, bd_cr, bd_ci, d_skip, *, steps):
    n, sw = u2.shape
    nb, ns = s0r.shape
    return pl.pallas_call(
        functools.partial(_ssm_sample_kernel, steps=steps),
        grid=(1,),
        in_specs=[_const_spec(u2.shape), _const_spec(s0r.shape), _const_spec(s0i.shape),
                  _const_spec(ab2.shape), _const_spec(bd_b.shape), _const_spec(bd_cr.shape),
                  _const_spec(bd_ci.shape), _const_spec((1, sw))],
        out_specs=[pl.BlockSpec((n, sw), lambda i: (0, 0)), pl.BlockSpec((nb, ns), lambda i: (0, 0)),
                   pl.BlockSpec((nb, ns), lambda i: (0, 0))],
        out_shape=[jax.ShapeDtypeStruct((n, sw), _F32), jax.ShapeDtypeStruct((nb, ns), _F32),
                   jax.ShapeDtypeStruct((nb, ns), _F32)],
        compiler_params=_params(),
        name="ssm_sample",
    )(u2, s0r, s0i, ab2, bd_b, bd_cr, bd_ci, d_skip.reshape(1, sw))


def _block_diag(blocks):
    g, r, c = blocks.shape
    eye = jnp.eye(g, dtype=blocks.dtype)
    return (blocks[:, :, None, :] * eye[:, None, :, None]).reshape(g * r, g * c)


def _tile_rows(n, target):
    tm = min(n, target)
    while n % tm:
        tm //= 2
    return tm


def _prompt_layer(x, w, prep):
    b, l, d = x.shape
    n = b * l
    x2 = x.reshape(n, d)
    tm = _tile_rows(l, 1024)
    tables = _rotary_tables(jnp.arange(l, dtype=jnp.int32))
    q, k, v, u_t = _inproj(x2, w["g_mix_pre"], w["wqkv"], w["wu_t"], tables, tm=tm,
                           u_transposed=True, q_dtype=_BF16, u_dtype=_BF16)
    aw, kvw = q.shape[1], k.shape[1]
    att = _attn_prompt(q.reshape(b, l, aw), k.reshape(b, l, kvw), v.reshape(b, l, kvw), w["sinks"],
                       blocks_per_step=_tile_rows(l // WINDOW, 4))
    sw = u_t.shape[0]
    kc = l // CHUNK
    m, bs, ca, _, apow, _ = prep
    g_t, states = _ssm_prompt(u_t, w["d_skip"], m, bs, ca, apow, kc)
    y = _mix_mlp(x2, att.reshape(n, aw), g_t, w["wglu_t"], w["b_glu"], w["wout"],
                 w["g_mix_post"], w["g_mlp_pre"], w["wup"], w["wdown"], w["g_mlp_post"],
                 tm=_tile_rows(n, 512), g_transposed=True)
    keep = min(WINDOW, l)
    k_state = k.reshape(b, l, kvw)[:, l - keep:].reshape(b, keep, N_KV_HEADS, HEAD_DIM)
    v_state = v.reshape(b, l, kvw)[:, l - keep:].reshape(b, keep, N_KV_HEADS, HEAD_DIM)
    fin = states[:, kc - 1::kc, :]
    s_re = jnp.swapaxes(fin[:, :, :SSM_STATE], 0, 1)
    s_im = jnp.swapaxes(fin[:, :, SSM_STATE:], 0, 1)
    return y.reshape(b, l, d), k_state, v_state, s_re, s_im


def _sample_layer(x, cache_k, cache_v, s0_re, s0_im, w, prep):
    nb, steps, d = x.shape
    n = nb * steps
    xs = jnp.swapaxes(x, 0, 1).reshape(n, d)
    pos = PAST_LEN + jnp.arange(steps, dtype=jnp.int32)
    tables = tuple(jnp.repeat(t, nb, axis=0) for t in _rotary_tables(pos))
    q, k, v, u = _inproj(xs, w["g_mix_pre"], w["wqkv"], w["wu"], tables, tm=n,
                         u_transposed=False, q_dtype=_F32, u_dtype=_F32)
    aw, kvw = q.shape[1], k.shape[1]
    win = cache_k.shape[1]
    att = _attn_sample(q.reshape(steps, nb, aw), k.reshape(steps, nb, kvw), v.reshape(steps, nb, kvw),
                       jnp.transpose(cache_k, (0, 2, 3, 1)), jnp.transpose(cache_v, (0, 2, 3, 1)),
                       w["sinks"], bb=_tile_rows(nb, 8))
    _, _, _, ab, _, bbt = prep
    p = SSM_STATE
    n_grp = ab.shape[0]
    ab2 = jnp.stack([ab[:, 0, :p].reshape(-1), ab[:, 0, p:].reshape(-1)])
    bd_b = jnp.concatenate([_block_diag(bbt[:, :, :p]), _block_diag(bbt[:, :, p:])], axis=1).astype(_BF16)
    bd_cr = _block_diag(jnp.swapaxes(w["c_re"], 1, 2)).astype(_BF16)
    bd_ci = _block_diag(jnp.swapaxes(w["c_im"], 1, 2)).astype(_BF16)
    gact, s_re, s_im = _ssm_sample(u, s0_re.reshape(nb, n_grp * p), s0_im.reshape(nb, n_grp * p),
                                   ab2, bd_b, bd_cr, bd_ci, w["d_skip"], steps=steps)
    y = _mix_mlp(xs, att.reshape(n, aw), gact, w["wglu"], w["b_glu"], w["wout"],
                 w["g_mix_post"], w["g_mlp_pre"], w["wup"], w["wdown"], w["g_mlp_post"],
                 tm=_tile_rows(n, 256), g_transposed=False)
    y = jnp.swapaxes(y.reshape(steps, nb, d), 0, 1)
    k_new = jnp.swapaxes(k.reshape(steps, nb, N_KV_HEADS, HEAD_DIM), 0, 1)
    v_new = jnp.swapaxes(v.reshape(steps, nb, N_KV_HEADS, HEAD_DIM), 0, 1)
    k_state = jnp.concatenate([cache_k, k_new], axis=1)[:, -win:]
    v_state = jnp.concatenate([cache_v, v_new], axis=1)[:, -win:]
    return y, k_state, v_state, s_re.reshape(nb, n_grp, p), s_im.reshape(nb, n_grp, p)


def kernel(x_prompt, x_sample, cache_k, cache_v, state_ssm_re, state_ssm_im, g_mix_pre, w_in, sinks,
           a_re, a_im, log_dt, b_re, b_im, c_re, c_im, d_skip, w_glu, b_glu, w_out, g_mix_post,
           g_mlp_pre, w_up, w_down, g_mlp_post):
    depth = w_in.shape[0]
    seq = x_prompt.shape[1]
    assert seq % CHUNK == 0 and seq % WINDOW == 0
    chunks_per_seq = seq // CHUNK
    n_pow = max(1, (chunks_per_seq - 1).bit_length())
    ssm_width = d_skip.shape[1]
    qkv_width = w_in.shape[2] - ssm_width
    yp, ys = x_prompt, x_sample
    outs = [[] for _ in range(8)]
    for l in range(depth):
        w = {
            "g_mix_pre": g_mix_pre[l], "sinks": sinks[l], "d_skip": d_skip[l], "b_glu": b_glu[l],
            "g_mix_post": g_mix_post[l], "g_mlp_pre": g_mlp_pre[l], "g_mlp_post": g_mlp_post[l],
            "c_re": c_re[l], "c_im": c_im[l],
            "wqkv": w_in[l, :, :qkv_width].astype(_BF16),
            "wu": w_in[l, :, qkv_width:].astype(_BF16),
            "wu_t": w_in[l, :, qkv_width:].T.astype(_BF16),
            "wglu": w_glu[l].astype(_BF16), "wglu_t": w_glu[l].T.astype(_BF16),
            "wout": w_out[l].astype(_BF16), "wup": w_up[l].astype(_BF16), "wdown": w_down[l].astype(_BF16),
        }
        prep = _ssm_prep(a_re[l], a_im[l], log_dt[l], b_re[l], b_im[l], c_re[l], c_im[l], n_pow)
        yp, kp, vp, srp, sip = _prompt_layer(yp, w, prep)
        ys, kss, vss, srs, sis = _sample_layer(ys, cache_k[l], cache_v[l], state_ssm_re[l],
                                               state_ssm_im[l], w, prep)
        for lst, val in zip(outs, (kp, vp, srp, sip, kss, vss, srs, sis)):
            lst.append(val)
    return (yp, ys) + tuple(jnp.stack(o) for o in outs)
```

```python
import functools
import math

import jax
import jax.numpy as jnp
import numpy as np
from jax import lax
from jax.experimental import pallas as pl
from jax.experimental.pallas import tpu as pltpu

HEAD_DIM = 64
N_KV_HEADS = 2
WINDOW = 128
ROT_DIM = HEAD_DIM // 4
ROPE_THETA = 500000.0
SSM_GROUP = 16
SSM_STATE = 64
NORM_EPS = 1e-6
NEG_INF = -1e30
PAST_LEN = 16384
CHUNK = 128
LANES = 128
VMEM_LIMIT = 56 * 1024 * 1024

_F32 = jnp.float32
_BF16 = jnp.bfloat16


def _params(n_parallel=1):
    return pltpu.CompilerParams(dimension_semantics=("arbitrary",) * n_parallel,
                                vmem_limit_bytes=VMEM_LIMIT)


def _rms(x, g):
    return x * lax.rsqrt(jnp.mean(x * x, axis=-1, keepdims=True) + NORM_EPS) * g


def _const_spec(shape):
    nd = len(shape)
    return pl.BlockSpec(shape, lambda *_: (0,) * nd, pipeline_mode=pl.Buffered(1))


def _rotary_tables(pos):
    half = ROT_DIM // 2
    inv_freq = np.power(ROPE_THETA, -np.arange(half, dtype=np.float64) * (2.0 / ROT_DIM))
    ang = pos.astype(np.float64)[:, None] * inv_freq[None, :]
    cos, sin = np.cos(ang), np.sin(ang)
    n = pos.shape[0]
    ones = np.ones((n, HEAD_DIM - ROT_DIM))
    zeros_h = np.zeros((n, half))
    zeros_r = np.zeros((n, HEAD_DIM - ROT_DIM))
    c_tab = np.concatenate([cos, cos, ones], axis=1)
    s_up = np.concatenate([-sin, zeros_h, zeros_r], axis=1)
    s_dn = np.concatenate([zeros_h, sin, zeros_r], axis=1)
    rep = LANES // HEAD_DIM
    return tuple(np.tile(t, (1, rep)).astype(np.float32) for t in (c_tab, s_up, s_dn))


def _inproj_kernel(x_ref, g_ref, wqkv_ref, wu_ref, c_ref, su_ref, sd_ref,
                   q_ref, k_ref, v_ref, u_ref, *, attn_width, kv_width, u_transposed):
    half = ROT_DIM // 2
    h = _rms(x_ref[...], g_ref[...]).astype(_BF16)
    qkv = jnp.dot(h, wqkv_ref[...], preferred_element_type=_F32)
    c_tab, s_up, s_dn = c_ref[...], su_ref[...], sd_ref[...]

    def rot(blk):
        return (blk * c_tab + pltpu.roll(blk, LANES - half, axis=1) * s_up
                + pltpu.roll(blk, half, axis=1) * s_dn)

    for j in range(attn_width // LANES):
        q_ref[:, j * LANES:(j + 1) * LANES] = rot(qkv[:, j * LANES:(j + 1) * LANES]).astype(q_ref.dtype)
    for j in range(kv_width // LANES):
        o = attn_width + j * LANES
        k_ref[:, j * LANES:(j + 1) * LANES] = rot(qkv[:, o:o + LANES])
    v_ref[...] = qkv[:, attn_width + kv_width:]
    if u_transposed:
        u = lax.dot_general(wu_ref[...], h, (((1,), (1,)), ((), ())), preferred_element_type=_F32)
    else:
        u = jnp.dot(h, wu_ref[...], preferred_element_type=_F32)
    u_ref[...] = u.astype(u_ref.dtype)


def _inproj(x2, g, wqkv, wu, tables, *, tm, u_transposed, q_dtype, u_dtype):
    n, d = x2.shape
    kv_width = N_KV_HEADS * HEAD_DIM
    attn_width = wqkv.shape[1] - 2 * kv_width
    ssm_width = wu.shape[0] if u_transposed else wu.shape[1]
    nt = n // tm
    tab_blocks = tables[0].shape[0] // tm
    tab_spec = pl.BlockSpec((tm, LANES), lambda i: (i % tab_blocks, 0))
    if u_transposed:
        u_shape, u_spec = (ssm_width, n), pl.BlockSpec((ssm_width, tm), lambda i: (0, i))
    else:
        u_shape, u_spec = (n, ssm_width), pl.BlockSpec((tm, ssm_width), lambda i: (i, 0))
    return pl.pallas_call(
        functools.partial(_inproj_kernel, attn_width=attn_width, kv_width=kv_width,
                          u_transposed=u_transposed),
        grid=(nt,),
        in_specs=[pl.BlockSpec((tm, d), lambda i: (i, 0)), _const_spec((1, d)),
                  _const_spec(wqkv.shape), _const_spec(wu.shape), tab_spec, tab_spec, tab_spec],
        out_specs=[pl.BlockSpec((tm, attn_width), lambda i: (i, 0)),
                   pl.BlockSpec((tm, kv_width), lambda i: (i, 0)),
                   pl.BlockSpec((tm, kv_width), lambda i: (i, 0)), u_spec],
        out_shape=[jax.ShapeDtypeStruct((n, attn_width), q_dtype),
                   jax.ShapeDtypeStruct((n, kv_width), _F32),
                   jax.ShapeDtypeStruct((n, kv_width), _F32),
                   jax.ShapeDtypeStruct(u_shape, u_dtype)],
        compiler_params=_params(),
        name="inproj_t" if u_transposed else "inproj_n",
    )(x2, g.reshape(1, d), wqkv, wu, *tables)


def _softmax_sink(s_parts, sink):
    m = sink
    for s in s_parts:
        m = jnp.maximum(m, jnp.max(s, axis=-1, keepdims=True))
    p_parts = [jnp.exp(s - m) for s in s_parts]
    denom = jnp.exp(sink - m)
    for p in p_parts:
        denom = denom + jnp.sum(p, axis=-1, keepdims=True)
    return [p / denom for p in p_parts]


def _attn_prompt_kernel(sink_ref, q_ref, kp_ref, kc_ref, vp_ref, vc_ref, o_ref, *, n_heads):
    blk = kp_ref.shape[0]
    n_sub = q_ref.shape[0] // blk
    first_step = pl.program_id(1) == 0
    heads_per_lane_block = LANES // HEAD_DIM
    kv_lane_blocks = n_heads // N_KV_HEADS // heads_per_lane_block
    row = lax.broadcasted_iota(jnp.int32, (blk, blk), 0)
    col = lax.broadcasted_iota(jnp.int32, (blk, blk), 1)
    own = col <= row
    lane = lax.broadcasted_iota(jnp.int32, (1, LANES), 1)
    scale = HEAD_DIM ** -0.5
    nt = (((1,), (1,)), ((), ()))
    k_all = jnp.concatenate([kp_ref[...], kc_ref[...]], axis=0)
    v_all = jnp.concatenate([vp_ref[...], vc_ref[...]], axis=0).astype(_BF16)
    k_swap = pltpu.roll(k_all, HEAD_DIM, axis=1)
    k_lo = [jnp.where(lane < HEAD_DIM, k_all if kv == 0 else k_swap, 0.0).astype(_BF16) for kv in range(N_KV_HEADS)]
    k_hi = [jnp.where(lane >= HEAD_DIM, k_swap if kv == 0 else k_all, 0.0).astype(_BF16) for kv in range(N_KV_HEADS)]
    for sub in range(n_sub):
        rows = slice(sub * blk, (sub + 1) * blk)
        keys = slice(sub * blk, (sub + 2) * blk)
        valid = (own | jnp.logical_not(first_step)) if sub == 0 else None
        p_rows, inv_denoms = [], []
        for kv in range(N_KV_HEADS):
            kk2 = jnp.concatenate([k_lo[kv][keys], k_hi[kv][keys]], axis=0)
            for pr in range(kv_lane_blocks):
                j = kv * kv_lane_blocks + pr
                qp = q_ref[rows, j * LANES:(j + 1) * LANES] * scale
                s2 = lax.dot_general(qp, kk2, nt, preferred_element_type=_F32)
                for e in range(heads_per_lane_block):
                    sink = sink_ref[j * heads_per_lane_block + e]
                    s_prev = s2[:, (2 * e) * blk:(2 * e + 1) * blk]
                    s_own = s2[:, (2 * e + 1) * blk:(2 * e + 2) * blk]
                    s = jnp.where(own, s_own, s_prev)
                    if valid is not None:
                        s = jnp.where(valid, s, NEG_INF)
                    m = jnp.maximum(jnp.max(s, axis=-1, keepdims=True), sink)
                    p = jnp.exp(s - m)
                    denom = jnp.sum(p, axis=-1, keepdims=True) + jnp.exp(sink - m)
                    p_rows.append(jnp.concatenate([jnp.where(own, 0.0, p), jnp.where(own, p, 0.0)],
                                                  axis=1).astype(_BF16))
                    inv_denoms.append(1.0 / denom)
        o_all = jnp.dot(jnp.concatenate(p_rows, axis=0), v_all[keys], preferred_element_type=_F32)
        for j in range(n_heads // heads_per_lane_block):
            kv = j // kv_lane_blocks
            parts = []
            for e in range(heads_per_lane_block):
                hd = j * heads_per_lane_block + e
                o_h = o_all[hd * blk:(hd + 1) * blk, :] * inv_denoms[hd]
                shift = (e - kv) * HEAD_DIM % LANES
                parts.append(pltpu.roll(o_h, shift, axis=1) if shift else o_h)
            out = parts[0]
            for e in range(1, heads_per_lane_block):
                out = jnp.where(lane >= e * HEAD_DIM, parts[e], out)
            o_ref[rows, j * LANES:(j + 1) * LANES] = out.astype(o_ref.dtype)


def _attn_prompt(q3, k3, v3, sinks, *, blocks_per_step):
    b, l, aw = q3.shape
    kvw = k3.shape[2]
    tq = blocks_per_step * WINDOW
    cur = lambda bi, i: (bi, i, 0)
    prev = lambda bi, i: (bi, jnp.maximum(i * blocks_per_step - 1, 0), 0)
    return pl.pallas_call(
        functools.partial(_attn_prompt_kernel, n_heads=aw // HEAD_DIM),
        grid=(b, l // tq),
        in_specs=[pl.BlockSpec(memory_space=pltpu.SMEM),
                  pl.BlockSpec((None, tq, aw), cur),
                  pl.BlockSpec((None, WINDOW, kvw), prev), pl.BlockSpec((None, tq, kvw), cur),
                  pl.BlockSpec((None, WINDOW, kvw), prev), pl.BlockSpec((None, tq, kvw), cur)],
        out_specs=pl.BlockSpec((None, tq, aw), cur),
        out_shape=jax.ShapeDtypeStruct((b, l, aw), _BF16),
        compiler_params=_params(2),
        name="attn_prompt",
    )(sinks, q3, k3, k3, v3, v3)


def _cmul(ar, ai, br, bi):
    return ar * br - ai * bi, ar * bi + ai * br


def _discretise(ar, ai, dt):
    mag = jnp.exp(ar * dt)
    ab_re, ab_im = mag * jnp.cos(ai * dt), mag * jnp.sin(ai * dt)
    den = ar * ar + ai * ai
    nr, ni = ab_re - 1.0, ab_im
    f_re = (nr * ar + ni * ai) / den
    f_im = (ni * ar - nr * ai) / den
    return ab_re, ab_im, f_re, f_im


def _pack(re, im):
    return jnp.concatenate([re, im], axis=1)


def _cmul_packed(a, s, s_swapped=None):
    p = a.shape[-1] // 2
    if s_swapped is None:
        s_swapped = pltpu.roll(s, p, axis=1)
    a_rr = jnp.concatenate([a[:, :p], a[:, :p]], axis=1)
    a_ii = jnp.concatenate([-a[:, p:], a[:, p:]], axis=1)
    return s * a_rr + s_swapped * a_ii


def _cpow_bits(base_re, base_im, tau, n_bits):
    out_re = out_im = None
    for bit in range(n_bits):
        on = ((tau >> bit) & 1) == 1
        f_re, f_im = jnp.where(on, base_re, 1.0), jnp.where(on, base_im, 0.0)
        out_re, out_im = (f_re, f_im) if out_re is None else _cmul(out_re, out_im, f_re, f_im)
        if bit + 1 < n_bits:
            base_re, base_im = _cmul(base_re, base_im, base_re, base_im)
    return out_re, out_im


def _ssm_prep_kernel(ldt_ref, a_re_ref, a_im_ref, bt_re_ref, bt_im_ref, c_re_ref, c_im_ref,
                     m_ref, bs_ref, ab_ref, apow_ref, bdb_ref, bdc_ref, kt_ref):
    t_len, p_len, c_len = CHUNK, SSM_STATE, SSM_GROUP
    dt = jnp.exp(ldt_ref[...])
    ab_re, ab_im, f_re, f_im = _discretise(a_re_ref[...], a_im_ref[...], dt)
    ab = _pack(ab_re, ab_im)
    ab_ref[...] = ab
    bb_re, bb_im = _cmul(f_re, f_im, bt_re_ref[...], bt_im_ref[...])
    bb = _pack(bb_re, bb_im)

    row = lax.broadcasted_iota(jnp.int32, (t_len, 2 * p_len), 0)
    lo = lax.broadcasted_iota(jnp.int32, (t_len, 2 * p_len), 1) < p_len
    n_bits = (t_len - 1).bit_length()
    pw_re, pw_im = _cpow_bits(_pack(ab_re, ab_re), _pack(ab_im, ab_im),
                              jnp.where(lo, row, t_len - 1 - row), n_bits)
    pw_re_sw, pw_im_sw = pltpu.roll(pw_re, p_len, axis=1), pltpu.roll(pw_im, p_len, axis=1)
    w_fwd, w_fwd_sw = jnp.where(lo, pw_re, pw_im_sw), jnp.where(lo, pw_im, pw_re_sw)
    w_rev, w_rev_sw = jnp.where(lo, pw_re_sw, pw_im), jnp.where(lo, pw_im_sw, pw_re)
    w_next = _cmul_packed(ab, w_fwd, w_fwd_sw)
    w_next_sw = _cmul_packed(ab, w_fwd_sw, -w_fwd)
    st_re, st_im = ab_re, ab_im
    for _ in range(n_bits):
        st_re, st_im = _cmul(st_re, st_im, st_re, st_im)
    rows = []
    for _ in range(apow_ref.shape[0]):
        rows.append(_pack(st_re, st_im))
        st_re, st_im = _cmul(st_re, st_im, st_re, st_im)
    apow_ref[...] = jnp.concatenate(rows, axis=0)

    c_re, c_im = c_re_ref[...], c_im_ref[...]
    g_rows = []
    for cp in range(c_len):
        g_re, g_im = _cmul(bb_re[cp:cp + 1, :], bb_im[cp:cp + 1, :], c_re, c_im)
        g_rows.append(_pack(g_re, -g_im))
    g2 = jnp.concatenate(g_rows, axis=0)
    kt_ref[...] = lax.dot_general(g2, w_rev, (((1,), (1,)), ((), ())), preferred_element_type=_F32,
                                  precision=lax.Precision.HIGHEST)

    t_idx = lax.broadcasted_iota(jnp.int32, (t_len, t_len), 0)
    s_idx = lax.broadcasted_iota(jnp.int32, (t_len, t_len), 1)
    causal = s_idx <= t_idx

    def toeplitz_rows(c, carry):
        for cp in range(c_len):
            taps = kt_ref[pl.ds(cp * c_len + c, 1), :]
            full = jnp.broadcast_to(taps, (t_len, t_len))
            skew = pltpu.roll(full, 1, axis=1, stride=1, stride_axis=0)
            m_ref[pl.ds(pl.multiple_of(c * t_len, t_len), t_len), cp * t_len:(cp + 1) * t_len] = (
                jnp.where(causal, skew, 0.0).astype(m_ref.dtype))
        return carry

    lax.fori_loop(0, c_len, toeplitz_rows, 0, unroll=2)

    for cp in range(c_len):
        bs_ref[cp * t_len:(cp + 1) * t_len, :] = _cmul_packed(bb[cp:cp + 1, :], w_rev, w_rev_sw).astype(bs_ref.dtype)

    lane = lax.broadcasted_iota(jnp.int32, (1, 2 * p_len), 1)
    conj = jnp.where(lane < p_len, 1.0, -1.0)
    c_pk = _pack(c_re, c_im)
    for c in range(c_len):
        m_ref[c * t_len:(c + 1) * t_len, c_len * t_len:] = (
            _cmul_packed(c_pk[c:c + 1, :], w_next, w_next_sw) * conj).astype(m_ref.dtype)

    n_state = bdb_ref.shape[1] // 2
    mine = lax.broadcasted_iota(jnp.int32, (c_len, n_state), 1) // p_len == pl.program_id(0)

    def spread(block):
        return jnp.where(mine, jnp.concatenate([block] * (n_state // p_len), axis=1), 0.0)

    bdb_ref[...] = _pack(spread(bb_re), spread(bb_im)).astype(bdb_ref.dtype)
    bdc_ref[...] = _pack(spread(c_re), spread(c_im)).astype(bdc_ref.dtype)


def _ssm_prep(a_re, a_im, log_dt, b_re, b_im, c_re, c_im, n_pow):
    g, p = a_re.shape
    c = b_re.shape[2]
    tw = c * CHUNK
    grp = lambda *shape: pl.BlockSpec((None,) + shape, lambda i: (i,) + (0,) * len(shape))
    rows = pl.BlockSpec((c, 2 * g * p), lambda i: (i, 0))
    return pl.pallas_call(
        _ssm_prep_kernel,
        grid=(g,),
        in_specs=[grp(1, 1), grp(1, p), grp(1, p), grp(c, p), grp(c, p), grp(c, p), grp(c, p)],
        out_specs=[grp(tw, tw + 2 * p), grp(tw, 2 * p), grp(1, 2 * p), grp(n_pow, 2 * p), rows, rows],
        out_shape=[jax.ShapeDtypeStruct((g, tw, tw + 2 * p), _BF16),
                   jax.ShapeDtypeStruct((g, tw, 2 * p), _BF16),
                   jax.ShapeDtypeStruct((g, 1, 2 * p), _F32),
                   jax.ShapeDtypeStruct((g, n_pow, 2 * p), _F32),
                   jax.ShapeDtypeStruct((g * c, 2 * g * p), _BF16),
                   jax.ShapeDtypeStruct((g * c, 2 * g * p), _BF16)],
        scratch_shapes=[pltpu.VMEM((c * c, CHUNK), _F32)],
        compiler_params=_params(),
        name="ssm_prep",
    )(log_dt.reshape(g, 1, 1), a_re.reshape(g, 1, p), a_im.reshape(g, 1, p),
      jnp.swapaxes(b_re, 1, 2), jnp.swapaxes(b_im, 1, 2), c_re, c_im)


def _ssm_prompt_kernel(d_ref, u_ref, mt_ref, bs_ref, apow_ref, g_ref, st_ref, *, chunks_per_seq, row_block):
    c_len, n_tok = u_ref.shape
    nc = n_tok // CHUNK
    grp = pl.program_id(0)
    u3 = u_ref[...].reshape(c_len, nc, CHUNK)
    x = jnp.concatenate([u3[c] for c in range(c_len)], axis=1)
    s = jnp.dot(x, bs_ref[...], preferred_element_type=_F32)
    k_idx = lax.broadcasted_iota(jnp.int32, s.shape, 0) % chunks_per_seq
    for j in range(apow_ref.shape[0]):
        d = 1 << j
        shifted = jnp.where(k_idx >= d, pltpu.roll(s, d, axis=0), 0.0)
        s = s + _cmul_packed(apow_ref[j:j + 1, :], shifted)
    st_ref[...] = s
    s_prev = jnp.where(k_idx >= 1, pltpu.roll(s, 1, axis=0), 0.0)
    xs_t = jnp.concatenate([u3[c].astype(_F32).T for c in range(c_len)] + [s_prev.T], axis=0).astype(_BF16)
    g_slabs = []
    for rb in range(c_len * CHUNK // row_block):
        y_t = jnp.dot(mt_ref[rb * row_block:(rb + 1) * row_block, :], xs_t, preferred_element_type=_F32)
        for ci in range(row_block // CHUNK):
            c = rb * (row_block // CHUNK) + ci
            yc = y_t[ci * CHUNK:(ci + 1) * CHUNK, :].T + d_ref[grp * c_len + c] * u3[c].astype(_F32)
            g_slabs.append(jax.nn.gelu(yc).astype(g_ref.dtype))
    g_ref[...] = jnp.stack(g_slabs, axis=0).reshape(c_len, n_tok)


def _ssm_prompt(u_t, d_skip, m, bs, apow, chunks_per_seq):
    width, n = u_t.shape
    g, tw, kw = m.shape
    c = width // g
    nc = n // CHUNK
    p2 = bs.shape[2]
    n_pow = apow.shape[1]
    grp = lambda *shape: pl.BlockSpec((None,) + shape, lambda i: (i,) + (0,) * len(shape))
    return pl.pallas_call(
        functools.partial(_ssm_prompt_kernel, chunks_per_seq=chunks_per_seq, row_block=min(tw, 512)),
        grid=(g,),
        in_specs=[pl.BlockSpec(memory_space=pltpu.SMEM),
                  pl.BlockSpec((c, n), lambda i: (i, 0)),
                  grp(tw, kw), grp(tw, p2), grp(n_pow, p2)],
        out_specs=[pl.BlockSpec((c, n), lambda i: (i, 0)), grp(nc, p2)],
        out_shape=[jax.ShapeDtypeStruct((width, n), _BF16),
                   jax.ShapeDtypeStruct((g, nc, p2), _F32)],
        compiler_params=_params(),
        name="ssm_prompt",
    )(d_skip, u_t, m, bs, apow)


def _mix_mlp_kernel(x_ref, att_ref, g_ref, wglu_ref, bglu_ref, wout_ref, gpost_ref, gpre_ref,
                    wup_ref, wdown_ref, gmlp_ref, o_ref, *, g_transposed, ff_chunk):
    aw = att_ref.shape[1]
    gact = g_ref[...]
    if g_transposed:
        z = jnp.dot(wglu_ref[...], gact, preferred_element_type=_F32) + bglu_ref[...]
    else:
        z = jnp.dot(gact.astype(_BF16), wglu_ref[...], preferred_element_type=_F32) + bglu_ref[...]
    ssm = (gact.astype(_F32) * (1.0 / (1.0 + jnp.exp(-z)))).astype(_BF16)
    mix = jnp.dot(att_ref[...], wout_ref[:aw, :], preferred_element_type=_F32)
    if g_transposed:
        mix = mix + lax.dot_general(ssm, wout_ref[aw:, :], (((0,), (0,)), ((), ())),
                                    preferred_element_type=_F32)
    else:
        mix = mix + jnp.dot(ssm, wout_ref[aw:, :], preferred_element_type=_F32)
    x1 = x_ref[...] + _rms(mix, gpost_ref[...])
    h = _rms(x1, gpre_ref[...]).astype(_BF16)
    d_ff = wup_ref.shape[1]
    f = None
    for j in range(d_ff // ff_chunk):
        up = jnp.dot(h, wup_ref[:, j * ff_chunk:(j + 1) * ff_chunk], preferred_element_type=_F32)
        act = jnp.square(jnp.maximum(up, 0.0)).astype(_BF16)
        part = jnp.dot(act, wdown_ref[j * ff_chunk:(j + 1) * ff_chunk, :], preferred_element_type=_F32)
        f = part if f is None else f + part
    o_ref[...] = x1 + _rms(f, gmlp_ref[...])


def _mix_mlp(x2, att, gact, wglu, bglu, wout, gpost, gpre, wup, wdown, gmlp, *, tm, g_transposed):
    n, d = x2.shape
    aw = att.shape[1]
    sw = wglu.shape[0]
    d_ff = wup.shape[1]
    if g_transposed:
        g_spec = pl.BlockSpec((sw, tm), lambda i: (0, i))
        bglu2 = bglu.reshape(sw, 1)
    else:
        g_spec = pl.BlockSpec((tm, sw), lambda i: (i, 0))
        bglu2 = bglu.reshape(1, sw)
    return pl.pallas_call(
        functools.partial(_mix_mlp_kernel, g_transposed=g_transposed, ff_chunk=min(d_ff, 1024)),
        grid=(n // tm,),
        in_specs=[pl.BlockSpec((tm, d), lambda i: (i, 0)), pl.BlockSpec((tm, aw), lambda i: (i, 0)), g_spec,
                  _const_spec(wglu.shape), _const_spec(bglu2.shape), _const_spec(wout.shape),
                  _const_spec((1, d)), _const_spec((1, d)), _const_spec(wup.shape),
                  _const_spec(wdown.shape), _const_spec((1, d))],
        out_specs=pl.BlockSpec((tm, d), lambda i: (i, 0)),
        out_shape=jax.ShapeDtypeStruct((n, d), _F32),
        compiler_params=_params(),
        name="mix_mlp_t" if g_transposed else "mix_mlp_n",
    )(x2, att, gact, wglu, bglu2, wout, gpost.reshape(1, d), gpre.reshape(1, d), wup, wdown,
      gmlp.reshape(1, d))


def _attn_sample_kernel(sink_ref, q_ref, kn_ref, vn_ref, kc_ref, vc_ref, o_ref, *, n_heads):
    steps, bb, _ = q_ref.shape
    win = kc_ref.shape[3]
    group = n_heads // N_KV_HEADS
    rows = group * steps * bb
    r_c = lax.broadcasted_iota(jnp.int32, (rows, bb * win), 0)
    c_c = lax.broadcasted_iota(jnp.int32, (rows, bb * win), 1)
    t_r, b_r = (r_c % (steps * bb)) // bb, r_c % bb
    mask_c = (c_c // win == b_r) & (c_c % win > t_r + win - WINDOW)
    r_n = lax.broadcasted_iota(jnp.int32, (rows, steps * bb), 0)
    c_n = lax.broadcasted_iota(jnp.int32, (rows, steps * bb), 1)
    t_rn, b_rn = (r_n % (steps * bb)) // bb, r_n % bb
    dt_n = t_rn - c_n // bb
    mask_n = (c_n % bb == b_rn) & (dt_n >= 0) & (dt_n < WINDOW)
    g_row = lax.broadcasted_iota(jnp.int32, (rows, 1), 0) // (steps * bb)
    scale = HEAD_DIM ** -0.5
    nt = (((1,), (1,)), ((), ()))
    q = q_ref[...].reshape(steps * bb, n_heads * HEAD_DIM)
    kn = kn_ref[...].reshape(steps * bb, N_KV_HEADS * HEAD_DIM)
    vn = vn_ref[...].reshape(steps * bb, N_KV_HEADS * HEAD_DIM)
    outs = []
    for kv in range(N_KV_HEADS):
        ls = slice(kv * HEAD_DIM, (kv + 1) * HEAD_DIM)
        qs = jnp.concatenate([q[:, (kv * group + gi) * HEAD_DIM:(kv * group + gi + 1) * HEAD_DIM]
                              for gi in range(group)], axis=0).astype(_BF16)
        sink = jnp.zeros((rows, 1), _F32)
        for gi in range(group):
            sink = jnp.where(g_row == gi, sink_ref[kv * group + gi], sink)
        kc_t = jnp.concatenate([kc_ref[b, kv] for b in range(bb)], axis=1).astype(_BF16)
        vc_t = jnp.concatenate([vc_ref[b, kv] for b in range(bb)], axis=1).astype(_BF16)
        s_c = jnp.dot(qs, kc_t, preferred_element_type=_F32) * scale
        s_n = lax.dot_general(qs, kn[:, ls].astype(_BF16), nt, preferred_element_type=_F32) * scale
        s_c = jnp.where(mask_c, s_c, NEG_INF)
        s_n = jnp.where(mask_n, s_n, NEG_INF)
        p_c, p_n = _softmax_sink([s_c, s_n], sink)
        o = (lax.dot_general(p_c.astype(_BF16), vc_t, nt, preferred_element_type=_F32)
             + jnp.dot(p_n.astype(_BF16), vn[:, ls].astype(_BF16), preferred_element_type=_F32))
        outs.extend(o[gi * steps * bb:(gi + 1) * steps * bb, :] for gi in range(group))
    o_ref[...] = jnp.concatenate(outs, axis=1).reshape(o_ref.shape).astype(o_ref.dtype)


def _attn_sample(q3, kn3, vn3, kc_t, vc_t, sinks, *, bb):
    steps, nb, aw = q3.shape
    kvw = kn3.shape[2]
    tok = lambda i: (0, i, 0)
    cache_spec = pl.BlockSpec((bb,) + kc_t.shape[1:], lambda i: (i, 0, 0, 0))
    return pl.pallas_call(
        functools.partial(_attn_sample_kernel, n_heads=aw // HEAD_DIM),
        grid=(nb // bb,),
        in_specs=[pl.BlockSpec(memory_space=pltpu.SMEM),
                  pl.BlockSpec((steps, bb, aw), tok), pl.BlockSpec((steps, bb, kvw), tok),
                  pl.BlockSpec((steps, bb, kvw), tok), cache_spec, cache_spec],
        out_specs=pl.BlockSpec((steps, bb, aw), tok),
        out_shape=jax.ShapeDtypeStruct((steps, nb, aw), _BF16),
        compiler_params=_params(),
        name="attn_sample",
    )(sinks, q3, kn3, vn3, kc_t, vc_t)


def _ssm_sample_kernel(u_ref, s0r_ref, s0i_ref, ab_ref, bdb_ref, bdc_ref, d_ref,
                       g_ref, sr_ref, si_ref, *, steps):
    nb = s0r_ref.shape[0]
    ns = s0r_ref.shape[1]
    nt = (((1,), (1,)), ((), ()))
    u = u_ref[...]
    bu = jnp.dot(u.astype(_BF16), bdb_ref[...], preferred_element_type=_F32)
    a_re, a_im = ab_ref[0:1, :], ab_ref[1:2, :]
    xr, xi = s0r_ref[...], s0i_ref[...]
    for t in range(steps):
        rs = slice(t * nb, (t + 1) * nb)
        nr = a_re * xr - a_im * xi + bu[rs, :ns]
        ni = a_re * xi + a_im * xr + bu[rs, ns:]
        xr, xi = nr, ni
        y = (lax.dot_general(xr.astype(_BF16), bdc_ref[:, :ns], nt, preferred_element_type=_F32)
             - lax.dot_general(xi.astype(_BF16), bdc_ref[:, ns:], nt, preferred_element_type=_F32))
        g_ref[rs, :] = jax.nn.gelu(y + d_ref[...] * u[rs, :])
    sr_ref[...] = xr
    si_ref[...] = xi


def _ssm_sample(u2, s0r, s0i, ab2, bd_b, bd_c, d_skip, *, steps):
    n, sw = u2.shape
    nb, ns = s0r.shape
    return pl.pallas_call(
        functools.partial(_ssm_sample_kernel, steps=steps),
        grid=(1,),
        in_specs=[_const_spec(u2.shape), _const_spec(s0r.shape), _const_spec(s0i.shape),
                  _const_spec(ab2.shape), _const_spec(bd_b.shape), _const_spec(bd_c.shape),
                  _const_spec((1, sw))],
        out_specs=[pl.BlockSpec((n, sw), lambda i: (0, 0)), pl.BlockSpec((nb, ns), lambda i: (0, 0)),
                   pl.BlockSpec((nb, ns), lambda i: (0, 0))],
        out_shape=[jax.ShapeDtypeStruct((n, sw), _F32), jax.ShapeDtypeStruct((nb, ns), _F32),
                   jax.ShapeDtypeStruct((nb, ns), _F32)],
        compiler_params=_params(),
        name="ssm_sample",
    )(u2, s0r, s0i, ab2, bd_b, bd_c, d_skip.reshape(1, sw))


def _tile_rows(n, target):
    tm = min(n, target)
    while n % tm:
        tm //= 2
    return tm


def _prompt_layer(x, w, prep):
    b, l, d = x.shape
    n = b * l
    x2 = x.reshape(n, d)
    tm = _tile_rows(l, 1024)
    tables = _rotary_tables(np.arange(l))
    q, k, v, u_t = _inproj(x2, w["g_mix_pre"], w["wqkv"], w["wu_t"], tables, tm=tm,
                           u_transposed=True, q_dtype=_BF16, u_dtype=_BF16)
    aw, kvw = q.shape[1], k.shape[1]
    att = _attn_prompt(q.reshape(b, l, aw), k.reshape(b, l, kvw), v.reshape(b, l, kvw), w["sinks"],
                       blocks_per_step=_tile_rows(l // WINDOW, 4))
    sw = u_t.shape[0]
    kc = l // CHUNK
    m, bs, _, apow, _, _ = prep
    g_t, states = _ssm_prompt(u_t, w["d_skip"], m, bs, apow, kc)
    y = _mix_mlp(x2, att.reshape(n, aw), g_t, w["wglu_t"], w["b_glu"], w["wout"],
                 w["g_mix_post"], w["g_mlp_pre"], w["wup"], w["wdown"], w["g_mlp_post"],
                 tm=_tile_rows(n, 512), g_transposed=True)
    keep = min(WINDOW, l)
    k_state = k.reshape(b, l, kvw)[:, l - keep:].reshape(b, keep, N_KV_HEADS, HEAD_DIM)
    v_state = v.reshape(b, l, kvw)[:, l - keep:].reshape(b, keep, N_KV_HEADS, HEAD_DIM)
    fin = states[:, kc - 1::kc, :]
    s_re = jnp.swapaxes(fin[:, :, :SSM_STATE], 0, 1)
    s_im = jnp.swapaxes(fin[:, :, SSM_STATE:], 0, 1)
    return y.reshape(b, l, d), k_state, v_state, s_re, s_im


def _sample_layer(x, cache_k, cache_v, s0_re, s0_im, w, prep):
    nb, steps, d = x.shape
    n = nb * steps
    xs = jnp.swapaxes(x, 0, 1).reshape(n, d)
    tables = tuple(np.repeat(t, nb, axis=0) for t in _rotary_tables(PAST_LEN + np.arange(steps)))
    q, k, v, u = _inproj(xs, w["g_mix_pre"], w["wqkv"], w["wu"], tables, tm=n,
                         u_transposed=False, q_dtype=_F32, u_dtype=_F32)
    aw, kvw = q.shape[1], k.shape[1]
    win = cache_k.shape[1]
    att = _attn_sample(q.reshape(steps, nb, aw), k.reshape(steps, nb, kvw), v.reshape(steps, nb, kvw),
                       jnp.transpose(cache_k, (0, 2, 3, 1)), jnp.transpose(cache_v, (0, 2, 3, 1)),
                       w["sinks"], bb=_tile_rows(nb, 8))
    _, _, ab, _, bd_b, bd_c = prep
    p = SSM_STATE
    n_grp = ab.shape[0]
    ab2 = jnp.stack([ab[:, 0, :p].reshape(-1), ab[:, 0, p:].reshape(-1)])
    gact, s_re, s_im = _ssm_sample(u, s0_re.reshape(nb, n_grp * p), s0_im.reshape(nb, n_grp * p),
                                   ab2, bd_b, bd_c, w["d_skip"], steps=steps)
    y = _mix_mlp(xs, att.reshape(n, aw), gact, w["wglu"], w["b_glu"], w["wout"],
                 w["g_mix_post"], w["g_mlp_pre"], w["wup"], w["wdown"], w["g_mlp_post"],
                 tm=_tile_rows(n, 256), g_transposed=False)
    y = jnp.swapaxes(y.reshape(steps, nb, d), 0, 1)
    k_new = jnp.swapaxes(k.reshape(steps, nb, N_KV_HEADS, HEAD_DIM), 0, 1)
    v_new = jnp.swapaxes(v.reshape(steps, nb, N_KV_HEADS, HEAD_DIM), 0, 1)
    k_state = jnp.concatenate([cache_k, k_new], axis=1)[:, -win:]
    v_state = jnp.concatenate([cache_v, v_new], axis=1)[:, -win:]
    return y, k_state, v_state, s_re.reshape(nb, n_grp, p), s_im.reshape(nb, n_grp, p)


def kernel(x_prompt, x_sample, cache_k, cache_v, state_ssm_re, state_ssm_im, g_mix_pre, w_in, sinks,
           a_re, a_im, log_dt, b_re, b_im, c_re, c_im, d_skip, w_glu, b_glu, w_out, g_mix_post,
           g_mlp_pre, w_up, w_down, g_mlp_post):
    depth = w_in.shape[0]
    seq = x_prompt.shape[1]
    assert seq % CHUNK == 0 and seq % WINDOW == 0
    chunks_per_seq = seq // CHUNK
    n_pow = max(1, (chunks_per_seq - 1).bit_length())
    ssm_width = d_skip.shape[1]
    qkv_width = w_in.shape[2] - ssm_width
    yp, ys = x_prompt, x_sample
    outs = [[] for _ in range(8)]
    for l in range(depth):
        w = {
            "g_mix_pre": g_mix_pre[l], "sinks": sinks[l], "d_skip": d_skip[l], "b_glu": b_glu[l],
            "g_mix_post": g_mix_post[l], "g_mlp_pre": g_mlp_pre[l], "g_mlp_post": g_mlp_post[l],
            "c_re": c_re[l], "c_im": c_im[l],
            "wqkv": w_in[l, :, :qkv_width].astype(_BF16),
            "wu": w_in[l, :, qkv_width:].astype(_BF16),
            "wu_t": w_in[l, :, qkv_width:].T.astype(_BF16),
            "wglu": w_glu[l].astype(_BF16), "wglu_t": w_glu[l].T.astype(_BF16),
            "wout": w_out[l].astype(_BF16), "wup": w_up[l].astype(_BF16), "wdown": w_down[l].astype(_BF16),
        }
        prep = _ssm_prep(a_re[l], a_im[l], log_dt[l], b_re[l], b_im[l], c_re[l], c_im[l], n_pow)
        yp, kp, vp, srp, sip = _prompt_layer(yp, w, prep)
        ys, kss, vss, srs, sis = _sample_layer(ys, cache_k[l], cache_v[l], state_ssm_re[l],
                                               state_ssm_im[l], w, prep)
        for lst, val in zip(outs, (kp, vp, srp, sip, kss, vss, srs, sis)):
            lst.append(val)
    return (yp, ys) + tuple(jnp.stack(o) for o in outs)
```

```python
import functools
import math

import jax
import jax.numpy as jnp
import numpy as np
from jax import lax
from jax.experimental import pallas as pl
from jax.experimental.pallas import tpu as pltpu

HEAD_DIM = 64
N_KV_HEADS = 2
WINDOW = 128
ROT_DIM = HEAD_DIM // 4
ROPE_THETA = 500000.0
SSM_GROUP = 16
SSM_STATE = 64
NORM_EPS = 1e-6
NEG_INF = -1e30
PAST_LEN = 16384
CHUNK = 128
LANES = 128
VMEM_LIMIT = 56 * 1024 * 1024

_F32 = jnp.float32
_BF16 = jnp.bfloat16


def _params(n_parallel=1):
    return pltpu.CompilerParams(dimension_semantics=("arbitrary",) * n_parallel,
                                vmem_limit_bytes=VMEM_LIMIT)


def _rms(x, g):
    return x * lax.rsqrt(jnp.mean(x * x, axis=-1, keepdims=True) + NORM_EPS) * g


def _const_spec(shape):
    nd = len(shape)
    return pl.BlockSpec(shape, lambda *_: (0,) * nd, pipeline_mode=pl.Buffered(1))


def _rotary_tables(pos):
    half = ROT_DIM // 2
    inv_freq = np.power(ROPE_THETA, -np.arange(half, dtype=np.float64) * (2.0 / ROT_DIM))
    ang = pos.astype(np.float64)[:, None] * inv_freq[None, :]
    cos, sin = np.cos(ang), np.sin(ang)
    n = pos.shape[0]
    ones = np.ones((n, HEAD_DIM - ROT_DIM))
    zeros_h = np.zeros((n, half))
    zeros_r = np.zeros((n, HEAD_DIM - ROT_DIM))
    c_tab = np.concatenate([cos, cos, ones], axis=1)
    s_up = np.concatenate([-sin, zeros_h, zeros_r], axis=1)
    s_dn = np.concatenate([zeros_h, sin, zeros_r], axis=1)
    rep = LANES // HEAD_DIM
    return tuple(np.tile(t, (1, rep)).astype(np.float32) for t in (c_tab, s_up, s_dn))


def _inproj_kernel(x_ref, g_ref, wqkv_ref, wu_ref, c_ref, su_ref, sd_ref,
                   q_ref, k_ref, v_ref, u_ref, *, attn_width, kv_width, u_transposed):
    half = ROT_DIM // 2
    h = _rms(x_ref[...], g_ref[...]).astype(_BF16)
    qkv = jnp.dot(h, wqkv_ref[...], preferred_element_type=_F32)
    c_tab, s_up, s_dn = c_ref[...], su_ref[...], sd_ref[...]

    def rot(blk):
        return (blk * c_tab + pltpu.roll(blk, LANES - half, axis=1) * s_up
                + pltpu.roll(blk, half, axis=1) * s_dn)

    for j in range(attn_width // LANES):
        q_ref[:, j * LANES:(j + 1) * LANES] = rot(qkv[:, j * LANES:(j + 1) * LANES]).astype(q_ref.dtype)
    for j in range(kv_width // LANES):
        o = attn_width + j * LANES
        k_ref[:, j * LANES:(j + 1) * LANES] = rot(qkv[:, o:o + LANES])
    v_ref[...] = qkv[:, attn_width + kv_width:]
    if u_transposed:
        u = lax.dot_general(wu_ref[...], h, (((1,), (1,)), ((), ())), preferred_element_type=_F32)
    else:
        u = jnp.dot(h, wu_ref[...], preferred_element_type=_F32)
    u_ref[...] = u.astype(u_ref.dtype)


def _inproj(x2, g, wqkv, wu, tables, *, tm, u_transposed, q_dtype, u_dtype):
    n, d = x2.shape
    kv_width = N_KV_HEADS * HEAD_DIM
    attn_width = wqkv.shape[1] - 2 * kv_width
    ssm_width = wu.shape[0] if u_transposed else wu.shape[1]
    nt = n // tm
    tab_blocks = tables[0].shape[0] // tm
    tab_spec = pl.BlockSpec((tm, LANES), lambda i: (i % tab_blocks, 0))
    if u_transposed:
        u_shape, u_spec = (ssm_width, n), pl.BlockSpec((ssm_width, tm), lambda i: (0, i))
    else:
        u_shape, u_spec = (n, ssm_width), pl.BlockSpec((tm, ssm_width), lambda i: (i, 0))
    return pl.pallas_call(
        functools.partial(_inproj_kernel, attn_width=attn_width, kv_width=kv_width,
                          u_transposed=u_transposed),
        grid=(nt,),
        in_specs=[pl.BlockSpec((tm, d), lambda i: (i, 0)), _const_spec((1, d)),
                  _const_spec(wqkv.shape), _const_spec(wu.shape), tab_spec, tab_spec, tab_spec],
        out_specs=[pl.BlockSpec((tm, attn_width), lambda i: (i, 0)),
                   pl.BlockSpec((tm, kv_width), lambda i: (i, 0)),
                   pl.BlockSpec((tm, kv_width), lambda i: (i, 0)), u_spec],
        out_shape=[jax.ShapeDtypeStruct((n, attn_width), q_dtype),
                   jax.ShapeDtypeStruct((n, kv_width), _F32),
                   jax.ShapeDtypeStruct((n, kv_width), _F32),
                   jax.ShapeDtypeStruct(u_shape, u_dtype)],
        compiler_params=_params(),
        name="inproj_t" if u_transposed else "inproj_n",
    )(x2, g.reshape(1, d), wqkv, wu, *tables)


def _softmax_sink(s_parts, sink):
    m = sink
    for s in s_parts:
        m = jnp.maximum(m, jnp.max(s, axis=-1, keepdims=True))
    p_parts = [jnp.exp(s - m) for s in s_parts]
    denom = jnp.exp(sink - m)
    for p in p_parts:
        denom = denom + jnp.sum(p, axis=-1, keepdims=True)
    return [p / denom for p in p_parts]


def _attn_prompt_kernel(sink_ref, q_ref, kp_ref, kc_ref, vp_ref, vc_ref, o_ref, *, n_heads):
    blk = kp_ref.shape[0]
    n_sub = q_ref.shape[0] // blk
    first_step = pl.program_id(1) == 0
    heads_per_lane_block = LANES // HEAD_DIM
    kv_lane_blocks = n_heads // N_KV_HEADS // heads_per_lane_block
    row = lax.broadcasted_iota(jnp.int32, (blk, blk), 0)
    col = lax.broadcasted_iota(jnp.int32, (blk, blk), 1)
    own = col <= row
    lane = lax.broadcasted_iota(jnp.int32, (1, LANES), 1)
    scale = HEAD_DIM ** -0.5
    nt = (((1,), (1,)), ((), ()))
    k_all = jnp.concatenate([kp_ref[...], kc_ref[...]], axis=0)
    v_all = jnp.concatenate([vp_ref[...], vc_ref[...]], axis=0).astype(_BF16)
    k_swap = pltpu.roll(k_all, HEAD_DIM, axis=1)
    k_dup = [jnp.where((lane < HEAD_DIM) == (kv == 0), k_all, k_swap).astype(_BF16) for kv in range(N_KV_HEADS)]
    for sub in range(n_sub):
        rows = slice(sub * blk, (sub + 1) * blk)
        keys = slice(sub * blk, (sub + 2) * blk)
        valid = (own | jnp.logical_not(first_step)) if sub == 0 else None
        p_rows, inv_denoms = [], []
        for kv in range(N_KV_HEADS):
            q_rows = []
            for pr in range(kv_lane_blocks):
                j = kv * kv_lane_blocks + pr
                qp = q_ref[rows, j * LANES:(j + 1) * LANES] * scale
                q_rows.extend(jnp.where(lane // HEAD_DIM == e, qp, 0.0).astype(qp.dtype)
                              for e in range(heads_per_lane_block))
            s_kv = lax.dot_general(jnp.concatenate(q_rows, axis=0), k_dup[kv][keys], nt,
                                   preferred_element_type=_F32)
            for hi in range(len(q_rows)):
                sink = sink_ref[kv * len(q_rows) + hi]
                s_prev = s_kv[hi * blk:(hi + 1) * blk, :blk]
                s_own = s_kv[hi * blk:(hi + 1) * blk, blk:]
                s = jnp.where(own, s_own, s_prev)
                if valid is not None:
                    s = jnp.where(valid, s, NEG_INF)
                m = jnp.maximum(jnp.max(s, axis=-1, keepdims=True), sink)
                p = jnp.exp(s - m)
                denom = jnp.sum(p, axis=-1, keepdims=True) + jnp.exp(sink - m)
                p_rows.append(jnp.concatenate([jnp.where(own, 0.0, p), jnp.where(own, p, 0.0)],
                                              axis=1).astype(_BF16))
                inv_denoms.append(1.0 / denom)
        o_all = jnp.dot(jnp.concatenate(p_rows, axis=0), v_all[keys], preferred_element_type=_F32)
        for j in range(n_heads // heads_per_lane_block):
            kv = j // kv_lane_blocks
            parts = []
            for e in range(heads_per_lane_block):
                hd = j * heads_per_lane_block + e
                o_h = o_all[hd * blk:(hd + 1) * blk, :] * inv_denoms[hd]
                shift = (e - kv) * HEAD_DIM % LANES
                parts.append(pltpu.roll(o_h, shift, axis=1) if shift else o_h)
            out = parts[0]
            for e in range(1, heads_per_lane_block):
                out = jnp.where(lane >= e * HEAD_DIM, parts[e], out)
            o_ref[rows, j * LANES:(j + 1) * LANES] = out.astype(o_ref.dtype)


def _attn_prompt(q3, k3, v3, sinks, *, blocks_per_step):
    b, l, aw = q3.shape
    kvw = k3.shape[2]
    tq = blocks_per_step * WINDOW
    cur = lambda bi, i: (bi, i, 0)
    prev = lambda bi, i: (bi, jnp.maximum(i * blocks_per_step - 1, 0), 0)
    return pl.pallas_call(
        functools.partial(_attn_prompt_kernel, n_heads=aw // HEAD_DIM),
        grid=(b, l // tq),
        in_specs=[pl.BlockSpec(memory_space=pltpu.SMEM),
                  pl.BlockSpec((None, tq, aw), cur),
                  pl.BlockSpec((None, WINDOW, kvw), prev), pl.BlockSpec((None, tq, kvw), cur),
                  pl.BlockSpec((None, WINDOW, kvw), prev), pl.BlockSpec((None, tq, kvw), cur)],
        out_specs=pl.BlockSpec((None, tq, aw), cur),
        out_shape=jax.ShapeDtypeStruct((b, l, aw), _BF16),
        compiler_params=_params(2),
        name="attn_prompt",
    )(sinks, q3, k3, k3, v3, v3)


def _cmul(ar, ai, br, bi):
    return ar * br - ai * bi, ar * bi + ai * br


def _discretise(ar, ai, dt):
    mag = jnp.exp(ar * dt)
    ab_re, ab_im = mag * jnp.cos(ai * dt), mag * jnp.sin(ai * dt)
    den = ar * ar + ai * ai
    nr, ni = ab_re - 1.0, ab_im
    f_re = (nr * ar + ni * ai) / den
    f_im = (ni * ar - nr * ai) / den
    return ab_re, ab_im, f_re, f_im


def _pack(re, im):
    return jnp.concatenate([re, im], axis=1)


def _cmul_packed(a, s, s_swapped=None):
    p = a.shape[-1] // 2
    if s_swapped is None:
        s_swapped = pltpu.roll(s, p, axis=1)
    a_rr = jnp.concatenate([a[:, :p], a[:, :p]], axis=1)
    a_ii = jnp.concatenate([-a[:, p:], a[:, p:]], axis=1)
    return s * a_rr + s_swapped * a_ii


def _cpow_bits(base_re, base_im, tau, n_bits):
    out_re = out_im = None
    for bit in range(n_bits):
        on = ((tau >> bit) & 1) == 1
        f_re, f_im = jnp.where(on, base_re, 1.0), jnp.where(on, base_im, 0.0)
        out_re, out_im = (f_re, f_im) if out_re is None else _cmul(out_re, out_im, f_re, f_im)
        if bit + 1 < n_bits:
            base_re, base_im = _cmul(base_re, base_im, base_re, base_im)
    return out_re, out_im


def _ssm_prep_kernel(ldt_ref, a_re_ref, a_im_ref, bt_re_ref, bt_im_ref, c_re_ref, c_im_ref,
                     m_ref, bs_ref, ab_ref, apow_ref, bdb_ref, bdc_ref, kt_ref):
    t_len, p_len, c_len = CHUNK, SSM_STATE, SSM_GROUP
    dt = jnp.exp(ldt_ref[...])
    ab_re, ab_im, f_re, f_im = _discretise(a_re_ref[...], a_im_ref[...], dt)
    ab = _pack(ab_re, ab_im)
    ab_ref[...] = ab
    bb_re, bb_im = _cmul(f_re, f_im, bt_re_ref[...], bt_im_ref[...])
    bb = _pack(bb_re, bb_im)

    row = lax.broadcasted_iota(jnp.int32, (t_len, 2 * p_len), 0)
    lo = lax.broadcasted_iota(jnp.int32, (t_len, 2 * p_len), 1) < p_len
    n_bits = (t_len - 1).bit_length()
    pw_re, pw_im = _cpow_bits(_pack(ab_re, ab_re), _pack(ab_im, ab_im),
                              jnp.where(lo, row, t_len - 1 - row), n_bits)
    pw_re_sw, pw_im_sw = pltpu.roll(pw_re, p_len, axis=1), pltpu.roll(pw_im, p_len, axis=1)
    w_fwd, w_fwd_sw = jnp.where(lo, pw_re, pw_im_sw), jnp.where(lo, pw_im, pw_re_sw)
    w_rev, w_rev_sw = jnp.where(lo, pw_re_sw, pw_im), jnp.where(lo, pw_im_sw, pw_re)
    w_next = _cmul_packed(ab, w_fwd, w_fwd_sw)
    w_next_sw = _cmul_packed(ab, w_fwd_sw, -w_fwd)
    st_re, st_im = ab_re, ab_im
    for _ in range(n_bits):
        st_re, st_im = _cmul(st_re, st_im, st_re, st_im)
    rows = []
    for _ in range(apow_ref.shape[0]):
        rows.append(_pack(st_re, st_im))
        st_re, st_im = _cmul(st_re, st_im, st_re, st_im)
    apow_ref[...] = jnp.concatenate(rows, axis=0)

    c_re, c_im = c_re_ref[...], c_im_ref[...]
    g_rows = []
    for cp in range(c_len):
        g_re, g_im = _cmul(bb_re[cp:cp + 1, :], bb_im[cp:cp + 1, :], c_re, c_im)
        g_rows.append(_pack(g_re, -g_im))
    g2 = jnp.concatenate(g_rows, axis=0)
    kt_ref[...] = lax.dot_general(g2, w_rev, (((1,), (1,)), ((), ())), preferred_element_type=_F32,
                                  precision=lax.Precision.HIGHEST)

    t_idx = lax.broadcasted_iota(jnp.int32, (t_len, t_len), 0)
    s_idx = lax.broadcasted_iota(jnp.int32, (t_len, t_len), 1)
    causal = s_idx <= t_idx

    def toeplitz_rows(c, carry):
        for cp in range(c_len):
            taps = kt_ref[pl.ds(cp * c_len + c, 1), :]
            full = jnp.broadcast_to(taps, (t_len, t_len))
            skew = pltpu.roll(full, 1, axis=1, stride=1, stride_axis=0)
            m_ref[pl.ds(pl.multiple_of(c * t_len, t_len), t_len), cp * t_len:(cp + 1) * t_len] = (
                jnp.where(causal, skew, 0.0).astype(m_ref.dtype))
        return carry

    lax.fori_loop(0, c_len, toeplitz_rows, 0, unroll=2)

    for cp in range(c_len):
        bs_ref[cp * t_len:(cp + 1) * t_len, :] = _cmul_packed(bb[cp:cp + 1, :], w_rev, w_rev_sw).astype(bs_ref.dtype)

    lane = lax.broadcasted_iota(jnp.int32, (1, 2 * p_len), 1)
    conj = jnp.where(lane < p_len, 1.0, -1.0)
    c_pk = _pack(c_re, c_im)
    for c in range(c_len):
        m_ref[c * t_len:(c + 1) * t_len, c_len * t_len:] = (
            _cmul_packed(c_pk[c:c + 1, :], w_next, w_next_sw) * conj).astype(m_ref.dtype)

    n_state = bdb_ref.shape[1] // 2
    mine = lax.broadcasted_iota(jnp.int32, (c_len, n_state), 1) // p_len == pl.program_id(0)

    def spread(block):
        return jnp.where(mine, jnp.concatenate([block] * (n_state // p_len), axis=1), 0.0)

    bdb_ref[...] = _pack(spread(bb_re), spread(bb_im)).astype(bdb_ref.dtype)
    bdc_ref[...] = _pack(spread(c_re), spread(c_im)).astype(bdc_ref.dtype)


def _ssm_prep(a_re, a_im, log_dt, b_re, b_im, c_re, c_im, n_pow):
    g, p = a_re.shape
    c = b_re.shape[2]
    tw = c * CHUNK
    grp = lambda *shape: pl.BlockSpec((None,) + shape, lambda i: (i,) + (0,) * len(shape))
    rows = pl.BlockSpec((c, 2 * g * p), lambda i: (i, 0))
    return pl.pallas_call(
        _ssm_prep_kernel,
        grid=(g,),
        in_specs=[grp(1, 1), grp(1, p), grp(1, p), grp(c, p), grp(c, p), grp(c, p), grp(c, p)],
        out_specs=[grp(tw, tw + 2 * p), grp(tw, 2 * p), grp(1, 2 * p), grp(n_pow, 2 * p), rows, rows],
        out_shape=[jax.ShapeDtypeStruct((g, tw, tw + 2 * p), _BF16),
                   jax.ShapeDtypeStruct((g, tw, 2 * p), _BF16),
                   jax.ShapeDtypeStruct((g, 1, 2 * p), _F32),
                   jax.ShapeDtypeStruct((g, n_pow, 2 * p), _F32),
                   jax.ShapeDtypeStruct((g * c, 2 * g * p), _BF16),
                   jax.ShapeDtypeStruct((g * c, 2 * g * p), _BF16)],
        scratch_shapes=[pltpu.VMEM((c * c, CHUNK), _F32)],
        compiler_params=_params(),
        name="ssm_prep",
    )(log_dt.reshape(g, 1, 1), a_re.reshape(g, 1, p), a_im.reshape(g, 1, p),
      jnp.swapaxes(b_re, 1, 2), jnp.swapaxes(b_im, 1, 2), c_re, c_im)


def _ssm_prompt_kernel(d_ref, u_ref, mt_ref, bs_ref, apow_ref, g_ref, st_ref, *, chunks_per_seq, row_block):
    c_len, n_tok = u_ref.shape
    nc = n_tok // CHUNK
    grp = pl.program_id(0)
    u3 = u_ref[...].reshape(c_len, nc, CHUNK)
    x = jnp.concatenate([u3[c] for c in range(c_len)], axis=1)
    s = jnp.dot(x, bs_ref[...], preferred_element_type=_F32)
    k_idx = lax.broadcasted_iota(jnp.int32, s.shape, 0) % chunks_per_seq
    for j in range(apow_ref.shape[0]):
        d = 1 << j
        shifted = jnp.where(k_idx >= d, pltpu.roll(s, d, axis=0), 0.0)
        s = s + _cmul_packed(apow_ref[j:j + 1, :], shifted)
    st_ref[...] = s
    s_prev = jnp.where(k_idx >= 1, pltpu.roll(s, 1, axis=0), 0.0)
    xs_t = jnp.concatenate([u3[c].astype(_F32).T for c in range(c_len)] + [s_prev.T], axis=0).astype(_BF16)
    g_slabs = []
    for rb in range(c_len * CHUNK // row_block):
        y_t = jnp.dot(mt_ref[rb * row_block:(rb + 1) * row_block, :], xs_t, preferred_element_type=_F32)
        for ci in range(row_block // CHUNK):
            c = rb * (row_block // CHUNK) + ci
            yc = y_t[ci * CHUNK:(ci + 1) * CHUNK, :].T + d_ref[grp * c_len + c] * u3[c].astype(_F32)
            g_slabs.append(jax.nn.gelu(yc).astype(g_ref.dtype))
    g_ref[...] = jnp.stack(g_slabs, axis=0).reshape(c_len, n_tok)


def _ssm_prompt(u_t, d_skip, m, bs, apow, chunks_per_seq):
    width, n = u_t.shape
    g, tw, kw = m.shape
    c = width // g
    nc = n // CHUNK
    p2 = bs.shape[2]
    n_pow = apow.shape[1]
    grp = lambda *shape: pl.BlockSpec((None,) + shape, lambda i: (i,) + (0,) * len(shape))
    return pl.pallas_call(
        functools.partial(_ssm_prompt_kernel, chunks_per_seq=chunks_per_seq, row_block=min(tw, 512)),
        grid=(g,),
        in_specs=[pl.BlockSpec(memory_space=pltpu.SMEM),
                  pl.BlockSpec((c, n), lambda i: (i, 0)),
                  grp(tw, kw), grp(tw, p2), grp(n_pow, p2)],
        out_specs=[pl.BlockSpec((c, n), lambda i: (i, 0)), grp(nc, p2)],
        out_shape=[jax.ShapeDtypeStruct((width, n), _BF16),
                   jax.ShapeDtypeStruct((g, nc, p2), _F32)],
        compiler_params=_params(),
        name="ssm_prompt",
    )(d_skip, u_t, m, bs, apow)


def _mix_mlp_kernel(x_ref, att_ref, g_ref, wglu_ref, bglu_ref, wout_ref, gpost_ref, gpre_ref,
                    wup_ref, wdown_ref, gmlp_ref, o_ref, *, g_transposed, ff_chunk, sub_rows):
    aw = att_ref.shape[1]
    d_ff = wup_ref.shape[1]

    def mix_stage(rows):
        if g_transposed:
            gact = g_ref[:, rows]
            z = jnp.dot(wglu_ref[...], gact, preferred_element_type=_F32) + bglu_ref[...]
        else:
            gact = g_ref[rows, :]
            z = jnp.dot(gact.astype(_BF16), wglu_ref[...], preferred_element_type=_F32) + bglu_ref[...]
        ssm = (gact.astype(_F32) * (1.0 / (1.0 + jnp.exp(-z)))).astype(_BF16)
        mix = jnp.dot(att_ref[rows, :], wout_ref[:aw, :], preferred_element_type=_F32)
        if g_transposed:
            mix = mix + lax.dot_general(ssm, wout_ref[aw:, :], (((0,), (0,)), ((), ())),
                                        preferred_element_type=_F32)
        else:
            mix = mix + jnp.dot(ssm, wout_ref[aw:, :], preferred_element_type=_F32)
        x1 = x_ref[rows, :] + _rms(mix, gpost_ref[...])
        return x1, _rms(x1, gpre_ref[...]).astype(_BF16)

    def mlp_stage(rows, x1, h):
        f = None
        for j in range(d_ff // ff_chunk):
            up = jnp.dot(h, wup_ref[:, j * ff_chunk:(j + 1) * ff_chunk], preferred_element_type=_F32)
            act = jnp.square(jnp.maximum(up, 0.0)).astype(_BF16)
            part = jnp.dot(act, wdown_ref[j * ff_chunk:(j + 1) * ff_chunk, :], preferred_element_type=_F32)
            f = part if f is None else f + part
        o_ref[rows, :] = x1 + _rms(f, gmlp_ref[...])

    n_sub = x_ref.shape[0] // sub_rows
    row_slices = [slice(s * sub_rows, (s + 1) * sub_rows) for s in range(n_sub)]
    staged = mix_stage(row_slices[0])
    for s in range(n_sub):
        nxt = mix_stage(row_slices[s + 1]) if s + 1 < n_sub else None
        mlp_stage(row_slices[s], *staged)
        staged = nxt


def _mix_mlp(x2, att, gact, wglu, bglu, wout, gpost, gpre, wup, wdown, gmlp, *, tm, g_transposed):
    n, d = x2.shape
    aw = att.shape[1]
    sw = wglu.shape[0]
    d_ff = wup.shape[1]
    if g_transposed:
        g_spec = pl.BlockSpec((sw, tm), lambda i: (0, i))
        bglu2 = bglu.reshape(sw, 1)
    else:
        g_spec = pl.BlockSpec((tm, sw), lambda i: (i, 0))
        bglu2 = bglu.reshape(1, sw)
    return pl.pallas_call(
        functools.partial(_mix_mlp_kernel, g_transposed=g_transposed, ff_chunk=min(d_ff, 1024),
                          sub_rows=min(tm, 512)),
        grid=(n // tm,),
        in_specs=[pl.BlockSpec((tm, d), lambda i: (i, 0)), pl.BlockSpec((tm, aw), lambda i: (i, 0)), g_spec,
                  _const_spec(wglu.shape), _const_spec(bglu2.shape), _const_spec(wout.shape),
                  _const_spec((1, d)), _const_spec((1, d)), _const_spec(wup.shape),
                  _const_spec(wdown.shape), _const_spec((1, d))],
        out_specs=pl.BlockSpec((tm, d), lambda i: (i, 0)),
        out_shape=jax.ShapeDtypeStruct((n, d), _F32),
        compiler_params=_params(),
        name="mix_mlp_t" if g_transposed else "mix_mlp_n",
    )(x2, att, gact, wglu, bglu2, wout, gpost.reshape(1, d), gpre.reshape(1, d), wup, wdown,
      gmlp.reshape(1, d))


def _attn_sample_kernel(sink_ref, q_ref, kn_ref, vn_ref, kc_ref, vc_ref, o_ref, *, n_heads):
    steps, bb, _ = q_ref.shape
    win = kc_ref.shape[3]
    group = n_heads // N_KV_HEADS
    rows = group * steps * bb
    r_c = lax.broadcasted_iota(jnp.int32, (rows, bb * win), 0)
    c_c = lax.broadcasted_iota(jnp.int32, (rows, bb * win), 1)
    t_r, b_r = (r_c % (steps * bb)) // bb, r_c % bb
    mask_c = (c_c // win == b_r) & (c_c % win > t_r + win - WINDOW)
    r_n = lax.broadcasted_iota(jnp.int32, (rows, steps * bb), 0)
    c_n = lax.broadcasted_iota(jnp.int32, (rows, steps * bb), 1)
    t_rn, b_rn = (r_n % (steps * bb)) // bb, r_n % bb
    dt_n = t_rn - c_n // bb
    mask_n = (c_n % bb == b_rn) & (dt_n >= 0) & (dt_n < WINDOW)
    g_row = lax.broadcasted_iota(jnp.int32, (rows, 1), 0) // (steps * bb)
    scale = HEAD_DIM ** -0.5
    nt = (((1,), (1,)), ((), ()))
    q = q_ref[...].reshape(steps * bb, n_heads * HEAD_DIM)
    kn = kn_ref[...].reshape(steps * bb, N_KV_HEADS * HEAD_DIM)
    vn = vn_ref[...].reshape(steps * bb, N_KV_HEADS * HEAD_DIM)
    outs = []
    for kv in range(N_KV_HEADS):
        ls = slice(kv * HEAD_DIM, (kv + 1) * HEAD_DIM)
        qs = jnp.concatenate([q[:, (kv * group + gi) * HEAD_DIM:(kv * group + gi + 1) * HEAD_DIM]
                              for gi in range(group)], axis=0).astype(_BF16)
        sink = jnp.zeros((rows, 1), _F32)
        for gi in range(group):
            sink = jnp.where(g_row == gi, sink_ref[kv * group + gi], sink)
        kc_t = jnp.concatenate([kc_ref[b, kv] for b in range(bb)], axis=1).astype(_BF16)
        vc_t = jnp.concatenate([vc_ref[b, kv] for b in range(bb)], axis=1).astype(_BF16)
        s_c = jnp.dot(qs, kc_t, preferred_element_type=_F32) * scale
        s_n = lax.dot_general(qs, kn[:, ls].astype(_BF16), nt, preferred_element_type=_F32) * scale
        s_c = jnp.where(mask_c, s_c, NEG_INF)
        s_n = jnp.where(mask_n, s_n, NEG_INF)
        p_c, p_n = _softmax_sink([s_c, s_n], sink)
        o = (lax.dot_general(p_c.astype(_BF16), vc_t, nt, preferred_element_type=_F32)
             + jnp.dot(p_n.astype(_BF16), vn[:, ls].astype(_BF16), preferred_element_type=_F32))
        outs.extend(o[gi * steps * bb:(gi + 1) * steps * bb, :] for gi in range(group))
    o_ref[...] = jnp.concatenate(outs, axis=1).reshape(o_ref.shape).astype(o_ref.dtype)


def _attn_sample(q3, kn3, vn3, kc_t, vc_t, sinks, *, bb):
    steps, nb, aw = q3.shape
    kvw = kn3.shape[2]
    tok = lambda i: (0, i, 0)
    cache_spec = pl.BlockSpec((bb,) + kc_t.shape[1:], lambda i: (i, 0, 0, 0))
    return pl.pallas_call(
        functools.partial(_attn_sample_kernel, n_heads=aw // HEAD_DIM),
        grid=(nb // bb,),
        in_specs=[pl.BlockSpec(memory_space=pltpu.SMEM),
                  pl.BlockSpec((steps, bb, aw), tok), pl.BlockSpec((steps, bb, kvw), tok),
                  pl.BlockSpec((steps, bb, kvw), tok), cache_spec, cache_spec],
        out_specs=pl.BlockSpec((steps, bb, aw), tok),
        out_shape=jax.ShapeDtypeStruct((steps, nb, aw), _BF16),
        compiler_params=_params(),
        name="attn_sample",
    )(sinks, q3, kn3, vn3, kc_t, vc_t)


def _ssm_sample_kernel(u_ref, s0r_ref, s0i_ref, ab_ref, bdb_ref, bdc_ref, d_ref,
                       g_ref, sr_ref, si_ref, *, steps):
    nb = s0r_ref.shape[0]
    ns = s0r_ref.shape[1]
    nt = (((1,), (1,)), ((), ()))
    u = u_ref[...]
    bu = jnp.dot(u.astype(_BF16), bdb_ref[...], preferred_element_type=_F32)
    a_re, a_im = ab_ref[0:1, :], ab_ref[1:2, :]
    xr, xi = s0r_ref[...], s0i_ref[...]
    for t in range(steps):
        rs = slice(t * nb, (t + 1) * nb)
        nr = a_re * xr - a_im * xi + bu[rs, :ns]
        ni = a_re * xi + a_im * xr + bu[rs, ns:]
        xr, xi = nr, ni
        y = (lax.dot_general(xr.astype(_BF16), bdc_ref[:, :ns], nt, preferred_element_type=_F32)
             - lax.dot_general(xi.astype(_BF16), bdc_ref[:, ns:], nt, preferred_element_type=_F32))
        g_ref[rs, :] = jax.nn.gelu(y + d_ref[...] * u[rs, :])
    sr_ref[...] = xr
    si_ref[...] = xi


def _ssm_sample(u2, s0r, s0i, ab2, bd_b, bd_c, d_skip, *, steps):
    n, sw = u2.shape
    nb, ns = s0r.shape
    return pl.pallas_call(
        functools.partial(_ssm_sample_kernel, steps=steps),
        grid=(1,),
        in_specs=[_const_spec(u2.shape), _const_spec(s0r.shape), _const_spec(s0i.shape),
                  _const_spec(ab2.shape), _const_spec(bd_b.shape), _const_spec(bd_c.shape),
                  _const_spec((1, sw))],
        out_specs=[pl.BlockSpec((n, sw), lambda i: (0, 0)), pl.BlockSpec((nb, ns), lambda i: (0, 0)),
                   pl.BlockSpec((nb, ns), lambda i: (0, 0))],
        out_shape=[jax.ShapeDtypeStruct((n, sw), _F32), jax.ShapeDtypeStruct((nb, ns), _F32),
                   jax.ShapeDtypeStruct((nb, ns), _F32)],
        compiler_params=_params(),
        name="ssm_sample",
    )(u2, s0r, s0i, ab2, bd_b, bd_c, d_skip.reshape(1, sw))


def _tile_rows(n, target):
    tm = min(n, target)
    while n % tm:
        tm //= 2
    return tm


def _prompt_layer(x, w, prep):
    b, l, d = x.shape
    n = b * l
    x2 = x.reshape(n, d)
    tm = _tile_rows(l, 1024)
    tables = _rotary_tables(np.arange(l))
    q, k, v, u_t = _inproj(x2, w["g_mix_pre"], w["wqkv"], w["wu_t"], tables, tm=tm,
                           u_transposed=True, q_dtype=_BF16, u_dtype=_BF16)
    aw, kvw = q.shape[1], k.shape[1]
    att = _attn_prompt(q.reshape(b, l, aw), k.reshape(b, l, kvw), v.reshape(b, l, kvw), w["sinks"],
                       blocks_per_step=_tile_rows(l // WINDOW, 4))
    sw = u_t.shape[0]
    kc = l // CHUNK
    m, bs, _, apow, _, _ = prep
    g_t, states = _ssm_prompt(u_t, w["d_skip"], m, bs, apow, kc)
    y = _mix_mlp(x2, att.reshape(n, aw), g_t, w["wglu_t"], w["b_glu"], w["wout"],
                 w["g_mix_post"], w["g_mlp_pre"], w["wup"], w["wdown"], w["g_mlp_post"],
                 tm=_tile_rows(n, 1024), g_transposed=True)
    keep = min(WINDOW, l)
    k_state = k.reshape(b, l, kvw)[:, l - keep:].reshape(b, keep, N_KV_HEADS, HEAD_DIM)
    v_state = v.reshape(b, l, kvw)[:, l - keep:].reshape(b, keep, N_KV_HEADS, HEAD_DIM)
    fin = states[:, kc - 1::kc, :]
    s_re = jnp.swapaxes(fin[:, :, :SSM_STATE], 0, 1)
    s_im = jnp.swapaxes(fin[:, :, SSM_STATE:], 0, 1)
    return y.reshape(b, l, d), k_state, v_state, s_re, s_im


def _sample_layer(x, cache_k, cache_v, s0_re, s0_im, w, prep):
    nb, steps, d = x.shape
    n = nb * steps
    xs = jnp.swapaxes(x, 0, 1).reshape(n, d)
    tables = tuple(np.repeat(t, nb, axis=0) for t in _rotary_tables(PAST_LEN + np.arange(steps)))
    q, k, v, u = _inproj(xs, w["g_mix_pre"], w["wqkv"], w["wu"], tables, tm=n,
                         u_transposed=False, q_dtype=_F32, u_dtype=_F32)
    aw, kvw = q.shape[1], k.shape[1]
    win = cache_k.shape[1]
    att = _attn_sample(q.reshape(steps, nb, aw), k.reshape(steps, nb, kvw), v.reshape(steps, nb, kvw),
                       jnp.transpose(cache_k, (0, 2, 3, 1)), jnp.transpose(cache_v, (0, 2, 3, 1)),
                       w["sinks"], bb=_tile_rows(nb, 8))
    _, _, ab, _, bd_b, bd_c = prep
    p = SSM_STATE
    n_grp = ab.shape[0]
    ab2 = jnp.stack([ab[:, 0, :p].reshape(-1), ab[:, 0, p:].reshape(-1)])
    gact, s_re, s_im = _ssm_sample(u, s0_re.reshape(nb, n_grp * p), s0_im.reshape(nb, n_grp * p),
                                   ab2, bd_b, bd_c, w["d_skip"], steps=steps)
    y = _mix_mlp(xs, att.reshape(n, aw), gact, w["wglu"], w["b_glu"], w["wout"],
                 w["g_mix_post"], w["g_mlp_pre"], w["wup"], w["wdown"], w["g_mlp_post"],
                 tm=_tile_rows(n, 256), g_transposed=False)
    y = jnp.swapaxes(y.reshape(steps, nb, d), 0, 1)
    k_new = jnp.swapaxes(k.reshape(steps, nb, N_KV_HEADS, HEAD_DIM), 0, 1)
    v_new = jnp.swapaxes(v.reshape(steps, nb, N_KV_HEADS, HEAD_DIM), 0, 1)
    k_state = jnp.concatenate([cache_k, k_new], axis=1)[:, -win:]
    v_state = jnp.concatenate([cache_v, v_new], axis=1)[:, -win:]
    return y, k_state, v_state, s_re.reshape(nb, n_grp, p), s_im.reshape(nb, n_grp, p)


def kernel(x_prompt, x_sample, cache_k, cache_v, state_ssm_re, state_ssm_im, g_mix_pre, w_in, sinks,
           a_re, a_im, log_dt, b_re, b_im, c_re, c_im, d_skip, w_glu, b_glu, w_out, g_mix_post,
           g_mlp_pre, w_up, w_down, g_mlp_post):
    depth = w_in.shape[0]
    seq = x_prompt.shape[1]
    assert seq % CHUNK == 0 and seq % WINDOW == 0
    chunks_per_seq = seq // CHUNK
    n_pow = max(1, (chunks_per_seq - 1).bit_length())
    ssm_width = d_skip.shape[1]
    qkv_width = w_in.shape[2] - ssm_width
    yp, ys = x_prompt, x_sample
    outs = [[] for _ in range(8)]
    for l in range(depth):
        w = {
            "g_mix_pre": g_mix_pre[l], "sinks": sinks[l], "d_skip": d_skip[l], "b_glu": b_glu[l],
            "g_mix_post": g_mix_post[l], "g_mlp_pre": g_mlp_pre[l], "g_mlp_post": g_mlp_post[l],
            "c_re": c_re[l], "c_im": c_im[l],
            "wqkv": w_in[l, :, :qkv_width].astype(_BF16),
            "wu": w_in[l, :, qkv_width:].astype(_BF16),
            "wu_t": w_in[l, :, qkv_width:].T.astype(_BF16),
            "wglu": w_glu[l].astype(_BF16), "wglu_t": w_glu[l].T.astype(_BF16),
            "wout": w_out[l].astype(_BF16), "wup": w_up[l].astype(_BF16), "wdown": w_down[l].astype(_BF16),
        }
        prep = _ssm_prep(a_re[l], a_im[l], log_dt[l], b_re[l], b_im[l], c_re[l], c_im[l], n_pow)
        yp, kp, vp, srp, sip = _prompt_layer(yp, w, prep)
        ys, kss, vss, srs, sis = _sample_layer(ys, cache_k[l], cache_v[l], state_ssm_re[l],
                                               state_ssm_im[l], w, prep)
        for lst, val in zip(outs, (kp, vp, srp, sip, kss, vss, srs, sis)):
            lst.append(val)
    return (yp, ys) + tuple(jnp.stack(o) for o in outs)
```

```python
import functools
import math

import jax
import jax.numpy as jnp
import numpy as np
from jax import lax
from jax.experimental import pallas as pl
from jax.experimental.pallas import tpu as pltpu

HEAD_DIM = 64
N_KV_HEADS = 2
WINDOW = 128
ROT_DIM = HEAD_DIM // 4
ROPE_THETA = 500000.0
SSM_GROUP = 16
SSM_STATE = 64
NORM_EPS = 1e-6
NEG_INF = -1e30
PAST_LEN = 16384
CHUNK = 128
LANES = 128
VMEM_LIMIT = 56 * 1024 * 1024

_F32 = jnp.float32
_BF16 = jnp.bfloat16


def _params(n_parallel=1):
    return pltpu.CompilerParams(dimension_semantics=("arbitrary",) * n_parallel,
                                vmem_limit_bytes=VMEM_LIMIT)


def _rms(x, g):
    return x * lax.rsqrt(jnp.mean(x * x, axis=-1, keepdims=True) + NORM_EPS) * g


def _const_spec(shape):
    nd = len(shape)
    return pl.BlockSpec(shape, lambda *_: (0,) * nd, pipeline_mode=pl.Buffered(1))


def _rotary_tables(pos):
    half = ROT_DIM // 2
    inv_freq = np.power(ROPE_THETA, -np.arange(half, dtype=np.float64) * (2.0 / ROT_DIM))
    ang = pos.astype(np.float64)[:, None] * inv_freq[None, :]
    cos, sin = np.cos(ang), np.sin(ang)
    n = pos.shape[0]
    ones = np.ones((n, HEAD_DIM - ROT_DIM))
    zeros_h = np.zeros((n, half))
    zeros_r = np.zeros((n, HEAD_DIM - ROT_DIM))
    c_tab = np.concatenate([cos, cos, ones], axis=1)
    s_up = np.concatenate([-sin, zeros_h, zeros_r], axis=1)
    s_dn = np.concatenate([zeros_h, sin, zeros_r], axis=1)
    rep = LANES // HEAD_DIM
    return tuple(np.tile(t, (1, rep)).astype(np.float32) for t in (c_tab, s_up, s_dn))


def _inproj_kernel(x_ref, g_ref, wqkv_ref, wu_ref, c_ref, su_ref, sd_ref,
                   q_ref, k_ref, v_ref, u_ref, *, attn_width, kv_width, u_transposed):
    half = ROT_DIM // 2
    h = _rms(x_ref[...], g_ref[...]).astype(_BF16)
    qkv = jnp.dot(h, wqkv_ref[...], preferred_element_type=_F32)
    c_tab, s_up, s_dn = c_ref[...], su_ref[...], sd_ref[...]

    def rot(blk):
        return (blk * c_tab + pltpu.roll(blk, LANES - half, axis=1) * s_up
                + pltpu.roll(blk, half, axis=1) * s_dn)

    for j in range(attn_width // LANES):
        q_ref[:, j * LANES:(j + 1) * LANES] = rot(qkv[:, j * LANES:(j + 1) * LANES]).astype(q_ref.dtype)
    for j in range(kv_width // LANES):
        o = attn_width + j * LANES
        k_ref[:, j * LANES:(j + 1) * LANES] = rot(qkv[:, o:o + LANES])
    v_ref[...] = qkv[:, attn_width + kv_width:]
    if u_transposed:
        u = lax.dot_general(wu_ref[...], h, (((1,), (1,)), ((), ())), preferred_element_type=_F32)
    else:
        u = jnp.dot(h, wu_ref[...], preferred_element_type=_F32)
    u_ref[...] = u.astype(u_ref.dtype)


def _inproj_parts(x2, g, wqkv, wu, tables, *, tm, u_transposed, q_dtype, u_dtype):
    n, d = x2.shape
    kv_width = N_KV_HEADS * HEAD_DIM
    attn_width = wqkv.shape[1] - 2 * kv_width
    ssm_width = wu.shape[0] if u_transposed else wu.shape[1]
    tab_blocks = tables[0].shape[0] // tm
    tab_spec = pl.BlockSpec((tm, LANES), lambda i: (i % tab_blocks, 0))
    if u_transposed:
        u_shape, u_spec = (ssm_width, n), pl.BlockSpec((ssm_width, tm), lambda i: (0, i))
    else:
        u_shape, u_spec = (n, ssm_width), pl.BlockSpec((tm, ssm_width), lambda i: (i, 0))
    return dict(
        kernel=functools.partial(_inproj_kernel, attn_width=attn_width, kv_width=kv_width,
                                 u_transposed=u_transposed),
        operands=[x2, g.reshape(1, d), wqkv, wu, *tables],
        in_specs=[pl.BlockSpec((tm, d), lambda i: (i, 0)), _const_spec((1, d)),
                  _const_spec(wqkv.shape), _const_spec(wu.shape), tab_spec, tab_spec, tab_spec],
        out_specs=[pl.BlockSpec((tm, attn_width), lambda i: (i, 0)),
                   pl.BlockSpec((tm, kv_width), lambda i: (i, 0)),
                   pl.BlockSpec((tm, kv_width), lambda i: (i, 0)), u_spec],
        out_shape=[jax.ShapeDtypeStruct((n, attn_width), q_dtype),
                   jax.ShapeDtypeStruct((n, kv_width), _F32),
                   jax.ShapeDtypeStruct((n, kv_width), _F32),
                   jax.ShapeDtypeStruct(u_shape, u_dtype)],
        scratch_shapes=[])


def _call_parts(parts, grid, name):
    return pl.pallas_call(parts["kernel"], grid=grid, in_specs=parts["in_specs"], out_specs=parts["out_specs"],
                          out_shape=parts["out_shape"], scratch_shapes=parts["scratch_shapes"],
                          compiler_params=_params(len(grid)), name=name)(*parts["operands"])


def _fused_kernel(*refs, first, second):
    a_in, a_out, a_scr = (len(first[k]) for k in ("in_specs", "out_specs", "scratch_shapes"))
    n_in = a_in + len(second["in_specs"])
    n_out = a_out + len(second["out_specs"])
    ins, outs, scr = refs[:n_in], refs[n_in:n_in + n_out], refs[n_in + n_out:]
    first["kernel"](*ins[:a_in], *outs[:a_out], *scr[:a_scr])
    second["kernel"](*ins[a_in:], *outs[a_out:], *scr[a_scr:])


def _call_fused(first, second, grid, name):
    outs = pl.pallas_call(
        functools.partial(_fused_kernel, first=first, second=second), grid=grid,
        in_specs=first["in_specs"] + second["in_specs"], out_specs=first["out_specs"] + second["out_specs"],
        out_shape=first["out_shape"] + second["out_shape"],
        scratch_shapes=first["scratch_shapes"] + second["scratch_shapes"],
        compiler_params=_params(len(grid)), name=name)(*first["operands"], *second["operands"])
    n_first = len(first["out_shape"])
    return outs[:n_first], outs[n_first:]


def _softmax_sink(s_parts, sink):
    m = sink
    for s in s_parts:
        m = jnp.maximum(m, jnp.max(s, axis=-1, keepdims=True))
    p_parts = [jnp.exp(s - m) for s in s_parts]
    denom = jnp.exp(sink - m)
    for p in p_parts:
        denom = denom + jnp.sum(p, axis=-1, keepdims=True)
    return [p / denom for p in p_parts]


def _attn_prompt_kernel(sink_ref, q_ref, kp_ref, kc_ref, vp_ref, vc_ref, o_ref, *, n_heads):
    blk = kp_ref.shape[0]
    n_sub = q_ref.shape[0] // blk
    first_step = pl.program_id(1) == 0
    heads_per_lane_block = LANES // HEAD_DIM
    kv_lane_blocks = n_heads // N_KV_HEADS // heads_per_lane_block
    row = lax.broadcasted_iota(jnp.int32, (blk, blk), 0)
    col = lax.broadcasted_iota(jnp.int32, (blk, blk), 1)
    own = col <= row
    lane = lax.broadcasted_iota(jnp.int32, (1, LANES), 1)
    scale = HEAD_DIM ** -0.5
    nt = (((1,), (1,)), ((), ()))
    k_all = jnp.concatenate([kp_ref[...], kc_ref[...]], axis=0)
    v_all = jnp.concatenate([vp_ref[...], vc_ref[...]], axis=0).astype(_BF16)
    k_swap = pltpu.roll(k_all, HEAD_DIM, axis=1)
    k_dup = [jnp.where((lane < HEAD_DIM) == (kv == 0), k_all, k_swap).astype(_BF16) for kv in range(N_KV_HEADS)]
    for sub in range(n_sub):
        rows = slice(sub * blk, (sub + 1) * blk)
        keys = slice(sub * blk, (sub + 2) * blk)
        valid = (own | jnp.logical_not(first_step)) if sub == 0 else None
        p_rows, inv_denoms = [], []
        for kv in range(N_KV_HEADS):
            q_rows = []
            for pr in range(kv_lane_blocks):
                j = kv * kv_lane_blocks + pr
                qp = q_ref[rows, j * LANES:(j + 1) * LANES] * scale
                q_rows.extend(jnp.where(lane // HEAD_DIM == e, qp, 0.0).astype(qp.dtype)
                              for e in range(heads_per_lane_block))
            s_kv = lax.dot_general(jnp.concatenate(q_rows, axis=0), k_dup[kv][keys], nt,
                                   preferred_element_type=_F32)
            for hi in range(len(q_rows)):
                sink = sink_ref[kv * len(q_rows) + hi]
                s_prev = s_kv[hi * blk:(hi + 1) * blk, :blk]
                s_own = s_kv[hi * blk:(hi + 1) * blk, blk:]
                s = jnp.where(own, s_own, s_prev)
                if valid is not None:
                    s = jnp.where(valid, s, NEG_INF)
                m = jnp.maximum(jnp.max(s, axis=-1, keepdims=True), sink)
                p = jnp.exp(s - m)
                denom = jnp.sum(p, axis=-1, keepdims=True) + jnp.exp(sink - m)
                p_rows.append(jnp.concatenate([jnp.where(own, 0.0, p), jnp.where(own, p, 0.0)],
                                              axis=1).astype(_BF16))
                inv_denoms.append(1.0 / denom)
        o_all = jnp.dot(jnp.concatenate(p_rows, axis=0), v_all[keys], preferred_element_type=_F32)
        for j in range(n_heads // heads_per_lane_block):
            kv = j // kv_lane_blocks
            parts = []
            for e in range(heads_per_lane_block):
                hd = j * heads_per_lane_block + e
                o_h = o_all[hd * blk:(hd + 1) * blk, :] * inv_denoms[hd]
                shift = (e - kv) * HEAD_DIM % LANES
                parts.append(pltpu.roll(o_h, shift, axis=1) if shift else o_h)
            out = parts[0]
            for e in range(1, heads_per_lane_block):
                out = jnp.where(lane >= e * HEAD_DIM, parts[e], out)
            o_ref[rows, j * LANES:(j + 1) * LANES] = out.astype(o_ref.dtype)


def _attn_prompt(q3, k3, v3, sinks, *, blocks_per_step):
    b, l, aw = q3.shape
    kvw = k3.shape[2]
    tq = blocks_per_step * WINDOW
    cur = lambda bi, i: (bi, i, 0)
    prev = lambda bi, i: (bi, jnp.maximum(i * blocks_per_step - 1, 0), 0)
    return pl.pallas_call(
        functools.partial(_attn_prompt_kernel, n_heads=aw // HEAD_DIM),
        grid=(b, l // tq),
        in_specs=[pl.BlockSpec(memory_space=pltpu.SMEM),
                  pl.BlockSpec((None, tq, aw), cur),
                  pl.BlockSpec((None, WINDOW, kvw), prev), pl.BlockSpec((None, tq, kvw), cur),
                  pl.BlockSpec((None, WINDOW, kvw), prev), pl.BlockSpec((None, tq, kvw), cur)],
        out_specs=pl.BlockSpec((None, tq, aw), cur),
        out_shape=jax.ShapeDtypeStruct((b, l, aw), _BF16),
        compiler_params=_params(2),
        name="attn_prompt",
    )(sinks, q3, k3, k3, v3, v3)


def _cmul(ar, ai, br, bi):
    return ar * br - ai * bi, ar * bi + ai * br


def _discretise(ar, ai, dt):
    mag = jnp.exp(ar * dt)
    ab_re, ab_im = mag * jnp.cos(ai * dt), mag * jnp.sin(ai * dt)
    den = ar * ar + ai * ai
    nr, ni = ab_re - 1.0, ab_im
    f_re = (nr * ar + ni * ai) / den
    f_im = (ni * ar - nr * ai) / den
    return ab_re, ab_im, f_re, f_im


def _pack(re, im):
    return jnp.concatenate([re, im], axis=1)


def _cmul_packed(a, s, s_swapped=None):
    p = a.shape[-1] // 2
    if s_swapped is None:
        s_swapped = pltpu.roll(s, p, axis=1)
    a_rr = jnp.concatenate([a[:, :p], a[:, :p]], axis=1)
    a_ii = jnp.concatenate([-a[:, p:], a[:, p:]], axis=1)
    return s * a_rr + s_swapped * a_ii


def _cpow_bits(base_re, base_im, tau, n_bits):
    out_re = out_im = None
    for bit in range(n_bits):
        on = ((tau >> bit) & 1) == 1
        f_re, f_im = jnp.where(on, base_re, 1.0), jnp.where(on, base_im, 0.0)
        out_re, out_im = (f_re, f_im) if out_re is None else _cmul(out_re, out_im, f_re, f_im)
        if bit + 1 < n_bits:
            base_re, base_im = _cmul(base_re, base_im, base_re, base_im)
    return out_re, out_im


def _ssm_prep_kernel(ldt_ref, a_re_ref, a_im_ref, bt_re_ref, bt_im_ref, c_re_ref, c_im_ref,
                     m_ref, bs_ref, ab_ref, apow_ref, bdb_ref, bdc_ref, kt_ref, *, straight_line):
    t_len, p_len, c_len = CHUNK, SSM_STATE, SSM_GROUP
    dt = jnp.exp(ldt_ref[...])
    ab_re, ab_im, f_re, f_im = _discretise(a_re_ref[...], a_im_ref[...], dt)
    ab = _pack(ab_re, ab_im)
    ab_ref[...] = ab
    bb_re, bb_im = _cmul(f_re, f_im, bt_re_ref[...], bt_im_ref[...])
    bb = _pack(bb_re, bb_im)

    row = lax.broadcasted_iota(jnp.int32, (t_len, 2 * p_len), 0)
    lo = lax.broadcasted_iota(jnp.int32, (t_len, 2 * p_len), 1) < p_len
    n_bits = (t_len - 1).bit_length()
    pw_re, pw_im = _cpow_bits(_pack(ab_re, ab_re), _pack(ab_im, ab_im),
                              jnp.where(lo, row, t_len - 1 - row), n_bits)
    pw_re_sw, pw_im_sw = pltpu.roll(pw_re, p_len, axis=1), pltpu.roll(pw_im, p_len, axis=1)
    w_fwd, w_fwd_sw = jnp.where(lo, pw_re, pw_im_sw), jnp.where(lo, pw_im, pw_re_sw)
    w_rev, w_rev_sw = jnp.where(lo, pw_re_sw, pw_im), jnp.where(lo, pw_im_sw, pw_re)
    w_next = _cmul_packed(ab, w_fwd, w_fwd_sw)
    w_next_sw = _cmul_packed(ab, w_fwd_sw, -w_fwd)
    st_re, st_im = ab_re, ab_im
    for _ in range(n_bits):
        st_re, st_im = _cmul(st_re, st_im, st_re, st_im)
    rows = []
    for _ in range(apow_ref.shape[0]):
        rows.append(_pack(st_re, st_im))
        st_re, st_im = _cmul(st_re, st_im, st_re, st_im)
    apow_ref[...] = jnp.concatenate(rows, axis=0)

    c_re, c_im = c_re_ref[...], c_im_ref[...]
    g_rows = []
    for cp in range(c_len):
        g_re, g_im = _cmul(bb_re[cp:cp + 1, :], bb_im[cp:cp + 1, :], c_re, c_im)
        g_rows.append(_pack(g_re, -g_im))
    g2 = jnp.concatenate(g_rows, axis=0)
    kt_ref[...] = lax.dot_general(g2, w_rev, (((1,), (1,)), ((), ())), preferred_element_type=_F32,
                                  precision=lax.Precision.HIGHEST)

    t_idx = lax.broadcasted_iota(jnp.int32, (t_len, t_len), 0)
    s_idx = lax.broadcasted_iota(jnp.int32, (t_len, t_len), 1)
    causal = s_idx <= t_idx

    def toeplitz_rows(c, carry):
        for cp in range(c_len):
            taps = kt_ref[pl.ds(cp * c_len + c, 1), :]
            full = jnp.broadcast_to(taps, (t_len, t_len))
            skew = pltpu.roll(full, 1, axis=1, stride=1, stride_axis=0)
            m_ref[pl.ds(pl.multiple_of(c * t_len, t_len), t_len), cp * t_len:(cp + 1) * t_len] = (
                jnp.where(causal, skew, 0.0).astype(m_ref.dtype))
        return carry

    if straight_line:
        for c in range(c_len):
            toeplitz_rows(c, 0)
    else:
        lax.fori_loop(0, c_len, toeplitz_rows, 0, unroll=2)

    for cp in range(c_len):
        bs_ref[cp * t_len:(cp + 1) * t_len, :] = _cmul_packed(bb[cp:cp + 1, :], w_rev, w_rev_sw).astype(bs_ref.dtype)

    lane = lax.broadcasted_iota(jnp.int32, (1, 2 * p_len), 1)
    conj = jnp.where(lane < p_len, 1.0, -1.0)
    c_pk = _pack(c_re, c_im)
    for c in range(c_len):
        m_ref[c * t_len:(c + 1) * t_len, c_len * t_len:] = (
            _cmul_packed(c_pk[c:c + 1, :], w_next, w_next_sw) * conj).astype(m_ref.dtype)

    n_state = bdb_ref.shape[1] // 2
    mine = lax.broadcasted_iota(jnp.int32, (c_len, n_state), 1) // p_len == pl.program_id(0)

    def spread(block):
        return jnp.where(mine, jnp.concatenate([block] * (n_state // p_len), axis=1), 0.0)

    bdb_ref[...] = _pack(spread(bb_re), spread(bb_im)).astype(bdb_ref.dtype)
    bdc_ref[...] = _pack(spread(c_re), spread(c_im)).astype(bdc_ref.dtype)


def _ssm_prep_parts(a_re, a_im, log_dt, b_re, b_im, c_re, c_im, n_pow, *, straight_line):
    g, p = a_re.shape
    c = b_re.shape[2]
    tw = c * CHUNK
    grp = lambda *shape: pl.BlockSpec((None,) + shape, lambda i: (i,) + (0,) * len(shape))
    rows = pl.BlockSpec((c, 2 * g * p), lambda i: (i, 0))
    return dict(
        kernel=functools.partial(_ssm_prep_kernel, straight_line=straight_line),
        operands=[log_dt.reshape(g, 1, 1), a_re.reshape(g, 1, p), a_im.reshape(g, 1, p),
                  jnp.swapaxes(b_re, 1, 2), jnp.swapaxes(b_im, 1, 2), c_re, c_im],
        in_specs=[grp(1, 1), grp(1, p), grp(1, p), grp(c, p), grp(c, p), grp(c, p), grp(c, p)],
        out_specs=[grp(tw, tw + 2 * p), grp(tw, 2 * p), grp(1, 2 * p), grp(n_pow, 2 * p), rows, rows],
        out_shape=[jax.ShapeDtypeStruct((g, tw, tw + 2 * p), _BF16),
                   jax.ShapeDtypeStruct((g, tw, 2 * p), _BF16),
                   jax.ShapeDtypeStruct((g, 1, 2 * p), _F32),
                   jax.ShapeDtypeStruct((g, n_pow, 2 * p), _F32),
                   jax.ShapeDtypeStruct((g * c, 2 * g * p), _BF16),
                   jax.ShapeDtypeStruct((g * c, 2 * g * p), _BF16)],
        scratch_shapes=[pltpu.VMEM((c * c, CHUNK), _F32)])


def _ssm_prompt_kernel(d_ref, u_ref, mt_ref, bs_ref, apow_ref, g_ref, st_ref, *, chunks_per_seq, row_block):
    n_grp = mt_ref.shape[0]
    c_len = u_ref.shape[0] // n_grp
    n_tok = u_ref.shape[1]
    nc = n_tok // CHUNK
    for gi in range(n_grp):
        grp = pl.program_id(0) * n_grp + gi
        u3 = u_ref[gi * c_len:(gi + 1) * c_len, :].reshape(c_len, nc, CHUNK)
        x = jnp.concatenate([u3[c] for c in range(c_len)], axis=1)
        s = jnp.dot(x, bs_ref[gi], preferred_element_type=_F32)
        k_idx = lax.broadcasted_iota(jnp.int32, s.shape, 0) % chunks_per_seq
        for j in range(apow_ref.shape[1]):
            d = 1 << j
            shifted = jnp.where(k_idx >= d, pltpu.roll(s, d, axis=0), 0.0)
            s = s + _cmul_packed(apow_ref[gi, j:j + 1, :], shifted)
        st_ref[gi] = s
        s_prev = jnp.where(k_idx >= 1, pltpu.roll(s, 1, axis=0), 0.0)
        xs_t = jnp.concatenate([u3[c].astype(_F32).T for c in range(c_len)] + [s_prev.T], axis=0).astype(_BF16)
        g_slabs = []
        for rb in range(c_len * CHUNK // row_block):
            y_t = jnp.dot(mt_ref[gi, rb * row_block:(rb + 1) * row_block, :], xs_t, preferred_element_type=_F32)
            for ci in range(row_block // CHUNK):
                c = rb * (row_block // CHUNK) + ci
                yc = y_t[ci * CHUNK:(ci + 1) * CHUNK, :].T + d_ref[grp * c_len + c] * u3[c].astype(_F32)
                g_slabs.append(jax.nn.gelu(yc).astype(g_ref.dtype))
        g_ref[gi * c_len:(gi + 1) * c_len, :] = jnp.stack(g_slabs, axis=0).reshape(c_len, n_tok)


def _ssm_prompt(u_t, d_skip, m, bs, apow, chunks_per_seq, *, groups_per_step):
    width, n = u_t.shape
    g, tw, kw = m.shape
    c = width // g
    nc = n // CHUNK
    p2 = bs.shape[2]
    n_pow = apow.shape[1]
    gps = groups_per_step
    grp = lambda *shape: pl.BlockSpec((gps,) + shape, lambda i: (i,) + (0,) * len(shape))
    return pl.pallas_call(
        functools.partial(_ssm_prompt_kernel, chunks_per_seq=chunks_per_seq, row_block=min(tw, 512)),
        grid=(g // gps,),
        in_specs=[pl.BlockSpec(memory_space=pltpu.SMEM),
                  pl.BlockSpec((gps * c, n), lambda i: (i, 0)),
                  grp(tw, kw), grp(tw, p2), grp(n_pow, p2)],
        out_specs=[pl.BlockSpec((gps * c, n), lambda i: (i, 0)), grp(nc, p2)],
        out_shape=[jax.ShapeDtypeStruct((width, n), _BF16),
                   jax.ShapeDtypeStruct((g, nc, p2), _F32)],
        compiler_params=_params(),
        name="ssm_prompt",
    )(d_skip, u_t, m, bs, apow)


def _mix_mlp_kernel(x_ref, att_ref, g_ref, wglu_ref, bglu_ref, wout_ref, gpost_ref, gpre_ref,
                    wup_ref, wdown_ref, gmlp_ref, o_ref, *, g_transposed, ff_chunk, sub_rows):
    aw = att_ref.shape[1]
    d_ff = wup_ref.shape[1]

    def mix_stage(rows):
        if g_transposed:
            gact = g_ref[:, rows]
            z = jnp.dot(wglu_ref[...], gact, preferred_element_type=_F32) + bglu_ref[...]
        else:
            gact = g_ref[rows, :]
            z = jnp.dot(gact.astype(_BF16), wglu_ref[...], preferred_element_type=_F32) + bglu_ref[...]
        ssm = (gact.astype(_F32) * (1.0 / (1.0 + jnp.exp(-z)))).astype(_BF16)
        mix = jnp.dot(att_ref[rows, :], wout_ref[:aw, :], preferred_element_type=_F32)
        if g_transposed:
            mix = mix + lax.dot_general(ssm, wout_ref[aw:, :], (((0,), (0,)), ((), ())),
                                        preferred_element_type=_F32)
        else:
            mix = mix + jnp.dot(ssm, wout_ref[aw:, :], preferred_element_type=_F32)
        x1 = x_ref[rows, :] + _rms(mix, gpost_ref[...])
        return x1, _rms(x1, gpre_ref[...]).astype(_BF16)

    def mlp_stage(rows, x1, h):
        f = None
        for j in range(d_ff // ff_chunk):
            up = jnp.dot(h, wup_ref[:, j * ff_chunk:(j + 1) * ff_chunk], preferred_element_type=_F32)
            act = jnp.square(jnp.maximum(up, 0.0)).astype(_BF16)
            part = jnp.dot(act, wdown_ref[j * ff_chunk:(j + 1) * ff_chunk, :], preferred_element_type=_F32)
            f = part if f is None else f + part
        o_ref[rows, :] = x1 + _rms(f, gmlp_ref[...])

    n_sub = x_ref.shape[0] // sub_rows
    row_slices = [slice(s * sub_rows, (s + 1) * sub_rows) for s in range(n_sub)]
    staged = mix_stage(row_slices[0])
    for s in range(n_sub):
        nxt = mix_stage(row_slices[s + 1]) if s + 1 < n_sub else None
        mlp_stage(row_slices[s], *staged)
        staged = nxt


def _mix_mlp(x2, att, gact, wglu, bglu, wout, gpost, gpre, wup, wdown, gmlp, *, tm, g_transposed):
    n, d = x2.shape
    aw = att.shape[1]
    sw = wglu.shape[0]
    d_ff = wup.shape[1]
    if g_transposed:
        g_spec = pl.BlockSpec((sw, tm), lambda i: (0, i))
        bglu2 = bglu.reshape(sw, 1)
    else:
        g_spec = pl.BlockSpec((tm, sw), lambda i: (i, 0))
        bglu2 = bglu.reshape(1, sw)
    return pl.pallas_call(
        functools.partial(_mix_mlp_kernel, g_transposed=g_transposed, ff_chunk=min(d_ff, 1024),
                          sub_rows=min(tm, 512)),
        grid=(n // tm,),
        in_specs=[pl.BlockSpec((tm, d), lambda i: (i, 0)), pl.BlockSpec((tm, aw), lambda i: (i, 0)), g_spec,
                  _const_spec(wglu.shape), _const_spec(bglu2.shape), _const_spec(wout.shape),
                  _const_spec((1, d)), _const_spec((1, d)), _const_spec(wup.shape),
                  _const_spec(wdown.shape), _const_spec((1, d))],
        out_specs=pl.BlockSpec((tm, d), lambda i: (i, 0)),
        out_shape=jax.ShapeDtypeStruct((n, d), _F32),
        compiler_params=_params(),
        name="mix_mlp_t" if g_transposed else "mix_mlp_n",
    )(x2, att, gact, wglu, bglu2, wout, gpost.reshape(1, d), gpre.reshape(1, d), wup, wdown,
      gmlp.reshape(1, d))


def _attn_sample_kernel(sink_ref, q_ref, kn_ref, vn_ref, kc_ref, vc_ref, o_ref, *, n_heads):
    steps, bb, _ = q_ref.shape
    win = kc_ref.shape[3]
    group = n_heads // N_KV_HEADS
    rows = group * steps * bb
    r_c = lax.broadcasted_iota(jnp.int32, (rows, bb * win), 0)
    c_c = lax.broadcasted_iota(jnp.int32, (rows, bb * win), 1)
    t_r, b_r = (r_c % (steps * bb)) // bb, r_c % bb
    mask_c = (c_c // win == b_r) & (c_c % win > t_r + win - WINDOW)
    r_n = lax.broadcasted_iota(jnp.int32, (rows, steps * bb), 0)
    c_n = lax.broadcasted_iota(jnp.int32, (rows, steps * bb), 1)
    t_rn, b_rn = (r_n % (steps * bb)) // bb, r_n % bb
    dt_n = t_rn - c_n // bb
    mask_n = (c_n % bb == b_rn) & (dt_n >= 0) & (dt_n < WINDOW)
    g_row = lax.broadcasted_iota(jnp.int32, (rows, 1), 0) // (steps * bb)
    scale = HEAD_DIM ** -0.5
    nt = (((1,), (1,)), ((), ()))
    q = q_ref[...].reshape(steps * bb, n_heads * HEAD_DIM)
    kn = kn_ref[...].reshape(steps * bb, N_KV_HEADS * HEAD_DIM)
    vn = vn_ref[...].reshape(steps * bb, N_KV_HEADS * HEAD_DIM)
    outs = []
    for kv in range(N_KV_HEADS):
        ls = slice(kv * HEAD_DIM, (kv + 1) * HEAD_DIM)
        qs = jnp.concatenate([q[:, (kv * group + gi) * HEAD_DIM:(kv * group + gi + 1) * HEAD_DIM]
                              for gi in range(group)], axis=0).astype(_BF16)
        sink = jnp.zeros((rows, 1), _F32)
        for gi in range(group):
            sink = jnp.where(g_row == gi, sink_ref[kv * group + gi], sink)
        kc_t = jnp.concatenate([kc_ref[b, kv] for b in range(bb)], axis=1).astype(_BF16)
        vc_t = jnp.concatenate([vc_ref[b, kv] for b in range(bb)], axis=1).astype(_BF16)
        s_c = jnp.dot(qs, kc_t, preferred_element_type=_F32) * scale
        s_n = lax.dot_general(qs, kn[:, ls].astype(_BF16), nt, preferred_element_type=_F32) * scale
        s_c = jnp.where(mask_c, s_c, NEG_INF)
        s_n = jnp.where(mask_n, s_n, NEG_INF)
        p_c, p_n = _softmax_sink([s_c, s_n], sink)
        o = (lax.dot_general(p_c.astype(_BF16), vc_t, nt, preferred_element_type=_F32)
             + jnp.dot(p_n.astype(_BF16), vn[:, ls].astype(_BF16), preferred_element_type=_F32))
        outs.extend(o[gi * steps * bb:(gi + 1) * steps * bb, :] for gi in range(group))
    o_ref[...] = jnp.concatenate(outs, axis=1).reshape(o_ref.shape).astype(o_ref.dtype)


def _attn_sample(q3, kn3, vn3, kc_t, vc_t, sinks, *, bb):
    steps, nb, aw = q3.shape
    kvw = kn3.shape[2]
    tok = lambda i: (0, i, 0)
    cache_spec = pl.BlockSpec((bb,) + kc_t.shape[1:], lambda i: (i, 0, 0, 0))
    return pl.pallas_call(
        functools.partial(_attn_sample_kernel, n_heads=aw // HEAD_DIM),
        grid=(nb // bb,),
        in_specs=[pl.BlockSpec(memory_space=pltpu.SMEM),
                  pl.BlockSpec((steps, bb, aw), tok), pl.BlockSpec((steps, bb, kvw), tok),
                  pl.BlockSpec((steps, bb, kvw), tok), cache_spec, cache_spec],
        out_specs=pl.BlockSpec((steps, bb, aw), tok),
        out_shape=jax.ShapeDtypeStruct((steps, nb, aw), _BF16),
        compiler_params=_params(),
        name="attn_sample",
    )(sinks, q3, kn3, vn3, kc_t, vc_t)


def _ssm_sample_kernel(u_ref, s0r_ref, s0i_ref, ab_ref, bdb_ref, bdc_ref, d_ref,
                       g_ref, sr_ref, si_ref, *, steps):
    nb = s0r_ref.shape[0]
    ns = s0r_ref.shape[1]
    nt = (((1,), (1,)), ((), ()))
    u = u_ref[...]
    bu = jnp.dot(u.astype(_BF16), bdb_ref[...], preferred_element_type=_F32)
    a_re, a_im = ab_ref[0:1, :], ab_ref[1:2, :]
    xr, xi = s0r_ref[...], s0i_ref[...]
    for t in range(steps):
        rs = slice(t * nb, (t + 1) * nb)
        nr = a_re * xr - a_im * xi + bu[rs, :ns]
        ni = a_re * xi + a_im * xr + bu[rs, ns:]
        xr, xi = nr, ni
        y = (lax.dot_general(xr.astype(_BF16), bdc_ref[:, :ns], nt, preferred_element_type=_F32)
             - lax.dot_general(xi.astype(_BF16), bdc_ref[:, ns:], nt, preferred_element_type=_F32))
        g_ref[rs, :] = jax.nn.gelu(y + d_ref[...] * u[rs, :])
    sr_ref[...] = xr
    si_ref[...] = xi


def _ssm_sample(u2, s0r, s0i, ab2, bd_b, bd_c, d_skip, *, steps):
    n, sw = u2.shape
    nb, ns = s0r.shape
    return pl.pallas_call(
        functools.partial(_ssm_sample_kernel, steps=steps),
        grid=(1,),
        in_specs=[_const_spec(u2.shape), _const_spec(s0r.shape), _const_spec(s0i.shape),
                  _const_spec(ab2.shape), _const_spec(bd_b.shape), _const_spec(bd_c.shape),
                  _const_spec((1, sw))],
        out_specs=[pl.BlockSpec((n, sw), lambda i: (0, 0)), pl.BlockSpec((nb, ns), lambda i: (0, 0)),
                   pl.BlockSpec((nb, ns), lambda i: (0, 0))],
        out_shape=[jax.ShapeDtypeStruct((n, sw), _F32), jax.ShapeDtypeStruct((nb, ns), _F32),
                   jax.ShapeDtypeStruct((nb, ns), _F32)],
        compiler_params=_params(),
        name="ssm_sample",
    )(u2, s0r, s0i, ab2, bd_b, bd_c, d_skip.reshape(1, sw))


def _tile_rows(n, target):
    tm = min(n, target)
    while n % tm:
        tm //= 2
    return tm


def _prompt_layer(x, w, ssm_weights, n_pow):
    b, l, d = x.shape
    n = b * l
    x2 = x.reshape(n, d)
    n_grp = ssm_weights[0].shape[0]
    tables = _rotary_tables(np.arange(l))
    tm = n // n_grp
    fuse = n % n_grp == 0 and tm % LANES == 0 and l % tm == 0 and tm <= 1024
    if not fuse:
        tm = _tile_rows(l, 1024)
    inproj = _inproj_parts(x2, w["g_mix_pre"], w["wqkv"], w["wu_t"], tables, tm=tm,
                           u_transposed=True, q_dtype=_BF16, u_dtype=_BF16)
    prep_parts = _ssm_prep_parts(*ssm_weights, n_pow, straight_line=fuse)
    if fuse:
        (q, k, v, u_t), prep = _call_fused(inproj, prep_parts, (n_grp,), "inproj_prep")
    else:
        q, k, v, u_t = _call_parts(inproj, (n // tm,), "inproj_t")
        prep = _call_parts(prep_parts, (n_grp,), "ssm_prep")
    aw, kvw = q.shape[1], k.shape[1]
    att = _attn_prompt(q.reshape(b, l, aw), k.reshape(b, l, kvw), v.reshape(b, l, kvw), w["sinks"],
                       blocks_per_step=_tile_rows(l // WINDOW, 4))
    sw = u_t.shape[0]
    kc = l // CHUNK
    m, bs, _, apow, _, _ = prep
    g_t, states = _ssm_prompt(u_t, w["d_skip"], m, bs, apow, kc, groups_per_step=_tile_rows(m.shape[0], 2))
    y = _mix_mlp(x2, att.reshape(n, aw), g_t, w["wglu_t"], w["b_glu"], w["wout"],
                 w["g_mix_post"], w["g_mlp_pre"], w["wup"], w["wdown"], w["g_mlp_post"],
                 tm=_tile_rows(n, 1024), g_transposed=True)
    keep = min(WINDOW, l)
    k_state = k.reshape(b, l, kvw)[:, l - keep:].reshape(b, keep, N_KV_HEADS, HEAD_DIM)
    v_state = v.reshape(b, l, kvw)[:, l - keep:].reshape(b, keep, N_KV_HEADS, HEAD_DIM)
    fin = states[:, kc - 1::kc, :]
    s_re = jnp.swapaxes(fin[:, :, :SSM_STATE], 0, 1)
    s_im = jnp.swapaxes(fin[:, :, SSM_STATE:], 0, 1)
    return (y.reshape(b, l, d), k_state, v_state, s_re, s_im), prep


def _sample_layer(x, cache_k, cache_v, s0_re, s0_im, w, prep):
    nb, steps, d = x.shape
    n = nb * steps
    xs = jnp.swapaxes(x, 0, 1).reshape(n, d)
    tables = tuple(np.repeat(t, nb, axis=0) for t in _rotary_tables(PAST_LEN + np.arange(steps)))
    q, k, v, u = _call_parts(_inproj_parts(xs, w["g_mix_pre"], w["wqkv"], w["wu"], tables, tm=n,
                                           u_transposed=False, q_dtype=_F32, u_dtype=_F32), (1,), "inproj_n")
    aw, kvw = q.shape[1], k.shape[1]
    win = cache_k.shape[1]
    att = _attn_sample(q.reshape(steps, nb, aw), k.reshape(steps, nb, kvw), v.reshape(steps, nb, kvw),
                       jnp.transpose(cache_k, (0, 2, 3, 1)), jnp.transpose(cache_v, (0, 2, 3, 1)),
                       w["sinks"], bb=_tile_rows(nb, 8))
    _, _, ab, _, bd_b, bd_c = prep
    p = SSM_STATE
    n_grp = ab.shape[0]
    ab2 = jnp.stack([ab[:, 0, :p].reshape(-1), ab[:, 0, p:].reshape(-1)])
    gact, s_re, s_im = _ssm_sample(u, s0_re.reshape(nb, n_grp * p), s0_im.reshape(nb, n_grp * p),
                                   ab2, bd_b, bd_c, w["d_skip"], steps=steps)
    y = _mix_mlp(xs, att.reshape(n, aw), gact, w["wglu"], w["b_glu"], w["wout"],
                 w["g_mix_post"], w["g_mlp_pre"], w["wup"], w["wdown"], w["g_mlp_post"],
                 tm=_tile_rows(n, 256), g_transposed=False)
    y = jnp.swapaxes(y.reshape(steps, nb, d), 0, 1)
    k_new = jnp.swapaxes(k.reshape(steps, nb, N_KV_HEADS, HEAD_DIM), 0, 1)
    v_new = jnp.swapaxes(v.reshape(steps, nb, N_KV_HEADS, HEAD_DIM), 0, 1)
    k_state = jnp.concatenate([cache_k, k_new], axis=1)[:, -win:]
    v_state = jnp.concatenate([cache_v, v_new], axis=1)[:, -win:]
    return y, k_state, v_state, s_re.reshape(nb, n_grp, p), s_im.reshape(nb, n_grp, p)


def kernel(x_prompt, x_sample, cache_k, cache_v, state_ssm_re, state_ssm_im, g_mix_pre, w_in, sinks,
           a_re, a_im, log_dt, b_re, b_im, c_re, c_im, d_skip, w_glu, b_glu, w_out, g_mix_post,
           g_mlp_pre, w_up, w_down, g_mlp_post):
    depth = w_in.shape[0]
    seq = x_prompt.shape[1]
    assert seq % CHUNK == 0 and seq % WINDOW == 0
    chunks_per_seq = seq // CHUNK
    n_pow = max(1, (chunks_per_seq - 1).bit_length())
    ssm_width = d_skip.shape[1]
    qkv_width = w_in.shape[2] - ssm_width
    yp, ys = x_prompt, x_sample
    outs = [[] for _ in range(8)]
    for l in range(depth):
        w = {
            "g_mix_pre": g_mix_pre[l], "sinks": sinks[l], "d_skip": d_skip[l], "b_glu": b_glu[l],
            "g_mix_post": g_mix_post[l], "g_mlp_pre": g_mlp_pre[l], "g_mlp_post": g_mlp_post[l],
            "c_re": c_re[l], "c_im": c_im[l],
            "wqkv": w_in[l, :, :qkv_width].astype(_BF16),
            "wu": w_in[l, :, qkv_width:].astype(_BF16),
            "wu_t": w_in[l, :, qkv_width:].T.astype(_BF16),
            "wglu": w_glu[l].astype(_BF16), "wglu_t": w_glu[l].T.astype(_BF16),
            "wout": w_out[l].astype(_BF16), "wup": w_up[l].astype(_BF16), "wdown": w_down[l].astype(_BF16),
        }
        (yp, kp, vp, srp, sip), prep = _prompt_layer(
            yp, w, (a_re[l], a_im[l], log_dt[l], b_re[l], b_im[l], c_re[l], c_im[l]), n_pow)
        ys, kss, vss, srs, sis = _sample_layer(ys, cache_k[l], cache_v[l], state_ssm_re[l],
                                               state_ssm_im[l], w, prep)
        for lst, val in zip(outs, (kp, vp, srp, sip, kss, vss, srs, sis)):
            lst.append(val)
    return (yp, ys) + tuple(jnp.stack(o) for o in outs)
```

```python
import functools
import math

import jax
import jax.numpy as jnp
import numpy as np
from jax import lax
from jax.experimental import pallas as pl
from jax.experimental.pallas import tpu as pltpu

HEAD_DIM = 64
N_KV_HEADS = 2
WINDOW = 128
ROT_DIM = HEAD_DIM // 4
ROPE_THETA = 500000.0
SSM_GROUP = 16
SSM_STATE = 64
NORM_EPS = 1e-6
NEG_INF = -1e30
PAST_LEN = 16384
CHUNK = 128
LANES = 128
VMEM_LIMIT = 56 * 1024 * 1024

_F32 = jnp.float32
_BF16 = jnp.bfloat16


def _params(n_parallel=1):
    return pltpu.CompilerParams(dimension_semantics=("arbitrary",) * n_parallel,
                                vmem_limit_bytes=VMEM_LIMIT)


def _rms(x, g):
    return x * lax.rsqrt(jnp.mean(x * x, axis=-1, keepdims=True) + NORM_EPS) * g


def _const_spec(shape):
    nd = len(shape)
    return pl.BlockSpec(shape, lambda *_: (0,) * nd, pipeline_mode=pl.Buffered(1))


def _rotary_tables(pos):
    half = ROT_DIM // 2
    inv_freq = np.power(ROPE_THETA, -np.arange(half, dtype=np.float64) * (2.0 / ROT_DIM))
    ang = pos.astype(np.float64)[:, None] * inv_freq[None, :]
    cos, sin = np.cos(ang), np.sin(ang)
    n = pos.shape[0]
    ones = np.ones((n, HEAD_DIM - ROT_DIM))
    zeros_h = np.zeros((n, half))
    zeros_r = np.zeros((n, HEAD_DIM - ROT_DIM))
    c_tab = np.concatenate([cos, cos, ones], axis=1)
    s_up = np.concatenate([-sin, zeros_h, zeros_r], axis=1)
    s_dn = np.concatenate([zeros_h, sin, zeros_r], axis=1)
    rep = LANES // HEAD_DIM
    return tuple(np.tile(t, (1, rep)).astype(np.float32) for t in (c_tab, s_up, s_dn))


def _inproj_kernel(x_ref, g_ref, wqkv_ref, wu_ref, c_ref, su_ref, sd_ref,
                   q_ref, k_ref, v_ref, u_ref, *, attn_width, kv_width, u_transposed):
    half = ROT_DIM // 2
    h = _rms(x_ref[...], g_ref[...]).astype(_BF16)
    qkv = jnp.dot(h, wqkv_ref[...], preferred_element_type=_F32)
    c_tab, s_up, s_dn = c_ref[...], su_ref[...], sd_ref[...]

    def rot(blk):
        return (blk * c_tab + pltpu.roll(blk, LANES - half, axis=1) * s_up
                + pltpu.roll(blk, half, axis=1) * s_dn)

    for j in range(attn_width // LANES):
        q_ref[:, j * LANES:(j + 1) * LANES] = rot(qkv[:, j * LANES:(j + 1) * LANES]).astype(q_ref.dtype)
    for j in range(kv_width // LANES):
        o = attn_width + j * LANES
        k_ref[:, j * LANES:(j + 1) * LANES] = rot(qkv[:, o:o + LANES])
    v_ref[...] = qkv[:, attn_width + kv_width:]
    if u_transposed:
        u = lax.dot_general(wu_ref[...], h, (((1,), (1,)), ((), ())), preferred_element_type=_F32)
    else:
        u = jnp.dot(h, wu_ref[...], preferred_element_type=_F32)
    u_ref[...] = u.astype(u_ref.dtype)


def _inproj_parts(x2, g, wqkv, wu, tables, *, tm, u_transposed, q_dtype, u_dtype):
    n, d = x2.shape
    kv_width = N_KV_HEADS * HEAD_DIM
    attn_width = wqkv.shape[1] - 2 * kv_width
    ssm_width = wu.shape[0] if u_transposed else wu.shape[1]
    tab_blocks = tables[0].shape[0] // tm
    tab_spec = pl.BlockSpec((tm, LANES), lambda i: (i % tab_blocks, 0))
    if u_transposed:
        u_shape, u_spec = (ssm_width, n), pl.BlockSpec((ssm_width, tm), lambda i: (0, i))
    else:
        u_shape, u_spec = (n, ssm_width), pl.BlockSpec((tm, ssm_width), lambda i: (i, 0))
    return dict(
        kernel=functools.partial(_inproj_kernel, attn_width=attn_width, kv_width=kv_width,
                                 u_transposed=u_transposed),
        operands=[x2, g.reshape(1, d), wqkv, wu, *tables],
        in_specs=[pl.BlockSpec((tm, d), lambda i: (i, 0)), _const_spec((1, d)),
                  _const_spec(wqkv.shape), _const_spec(wu.shape), tab_spec, tab_spec, tab_spec],
        out_specs=[pl.BlockSpec((tm, attn_width), lambda i: (i, 0)),
                   pl.BlockSpec((tm, kv_width), lambda i: (i, 0)),
                   pl.BlockSpec((tm, kv_width), lambda i: (i, 0)), u_spec],
        out_shape=[jax.ShapeDtypeStruct((n, attn_width), q_dtype),
                   jax.ShapeDtypeStruct((n, kv_width), _F32),
                   jax.ShapeDtypeStruct((n, kv_width), _F32),
                   jax.ShapeDtypeStruct(u_shape, u_dtype)],
        scratch_shapes=[])


def _call_parts(parts, grid, name):
    return pl.pallas_call(parts["kernel"], grid=grid, in_specs=parts["in_specs"], out_specs=parts["out_specs"],
                          out_shape=parts["out_shape"], scratch_shapes=parts["scratch_shapes"],
                          compiler_params=_params(len(grid)), name=name)(*parts["operands"])


def _fused_kernel(*refs, first, second):
    a_in, a_out, a_scr = (len(first[k]) for k in ("in_specs", "out_specs", "scratch_shapes"))
    n_in = a_in + len(second["in_specs"])
    n_out = a_out + len(second["out_specs"])
    ins, outs, scr = refs[:n_in], refs[n_in:n_in + n_out], refs[n_in + n_out:]
    first["kernel"](*ins[:a_in], *outs[:a_out], *scr[:a_scr])
    second["kernel"](*ins[a_in:], *outs[a_out:], *scr[a_scr:])


def _call_fused(first, second, grid, name):
    outs = pl.pallas_call(
        functools.partial(_fused_kernel, first=first, second=second), grid=grid,
        in_specs=first["in_specs"] + second["in_specs"], out_specs=first["out_specs"] + second["out_specs"],
        out_shape=first["out_shape"] + second["out_shape"],
        scratch_shapes=first["scratch_shapes"] + second["scratch_shapes"],
        compiler_params=_params(len(grid)), name=name)(*first["operands"], *second["operands"])
    n_first = len(first["out_shape"])
    return outs[:n_first], outs[n_first:]


def _softmax_sink(s_parts, sink):
    m = sink
    for s in s_parts:
        m = jnp.maximum(m, jnp.max(s, axis=-1, keepdims=True))
    p_parts = [jnp.exp(s - m) for s in s_parts]
    denom = jnp.exp(sink - m)
    for p in p_parts:
        denom = denom + jnp.sum(p, axis=-1, keepdims=True)
    return [p / denom for p in p_parts]


def _attn_prompt_kernel(sink_ref, q_ref, kp_ref, kc_ref, vp_ref, vc_ref, o_ref, *, n_heads):
    blk = kp_ref.shape[0]
    n_sub = q_ref.shape[0] // blk
    first_step = pl.program_id(1) == 0
    heads_per_lane_block = LANES // HEAD_DIM
    kv_lane_blocks = n_heads // N_KV_HEADS // heads_per_lane_block
    row = lax.broadcasted_iota(jnp.int32, (blk, blk), 0)
    col = lax.broadcasted_iota(jnp.int32, (blk, blk), 1)
    own = col <= row
    lane = lax.broadcasted_iota(jnp.int32, (1, LANES), 1)
    scale = HEAD_DIM ** -0.5
    nt = (((1,), (1,)), ((), ()))
    k_all = jnp.concatenate([kp_ref[...], kc_ref[...]], axis=0)
    v_all = jnp.concatenate([vp_ref[...], vc_ref[...]], axis=0).astype(_BF16)
    k_swap = pltpu.roll(k_all, HEAD_DIM, axis=1)
    k_dup = [jnp.where((lane < HEAD_DIM) == (kv == 0), k_all, k_swap).astype(_BF16) for kv in range(N_KV_HEADS)]
    for sub in range(n_sub):
        rows = slice(sub * blk, (sub + 1) * blk)
        keys = slice(sub * blk, (sub + 2) * blk)
        valid = (own | jnp.logical_not(first_step)) if sub == 0 else None
        for kv in range(N_KV_HEADS):
            p_rows, inv_denoms = [], []
            q_rows = []
            for pr in range(kv_lane_blocks):
                j = kv * kv_lane_blocks + pr
                qp = q_ref[rows, j * LANES:(j + 1) * LANES] * scale
                q_rows.extend(jnp.where(lane // HEAD_DIM == e, qp, 0.0).astype(qp.dtype)
                              for e in range(heads_per_lane_block))
            s_kv = lax.dot_general(jnp.concatenate(q_rows, axis=0), k_dup[kv][keys], nt,
                                   preferred_element_type=_F32)
            for hi in range(len(q_rows)):
                sink = sink_ref[kv * len(q_rows) + hi]
                s_prev = s_kv[hi * blk:(hi + 1) * blk, :blk]
                s_own = s_kv[hi * blk:(hi + 1) * blk, blk:]
                s = jnp.where(own, s_own, s_prev)
                if valid is not None:
                    s = jnp.where(valid, s, NEG_INF)
                m = jnp.maximum(jnp.max(s, axis=-1, keepdims=True), sink)
                p = jnp.exp(s - m)
                denom = jnp.sum(p, axis=-1, keepdims=True) + jnp.exp(sink - m)
                p_rows.append(jnp.concatenate([jnp.where(own, 0.0, p), jnp.where(own, p, 0.0)],
                                              axis=1).astype(_BF16))
                inv_denoms.append(1.0 / denom)
            o_kv = jnp.dot(jnp.concatenate(p_rows, axis=0), v_all[keys], preferred_element_type=_F32)
            for pr in range(kv_lane_blocks):
                parts = []
                for e in range(heads_per_lane_block):
                    hi = pr * heads_per_lane_block + e
                    o_h = o_kv[hi * blk:(hi + 1) * blk, :] * inv_denoms[hi]
                    shift = (e - kv) * HEAD_DIM % LANES
                    parts.append(pltpu.roll(o_h, shift, axis=1) if shift else o_h)
                out = parts[0]
                for e in range(1, heads_per_lane_block):
                    out = jnp.where(lane >= e * HEAD_DIM, parts[e], out)
                j = kv * kv_lane_blocks + pr
                o_ref[rows, j * LANES:(j + 1) * LANES] = out.astype(o_ref.dtype)


def _attn_prompt(q3, k3, v3, sinks, *, blocks_per_step):
    b, l, aw = q3.shape
    kvw = k3.shape[2]
    tq = blocks_per_step * WINDOW
    cur = lambda bi, i: (bi, i, 0)
    prev = lambda bi, i: (bi, jnp.maximum(i * blocks_per_step - 1, 0), 0)
    return pl.pallas_call(
        functools.partial(_attn_prompt_kernel, n_heads=aw // HEAD_DIM),
        grid=(b, l // tq),
        in_specs=[pl.BlockSpec(memory_space=pltpu.SMEM),
                  pl.BlockSpec((None, tq, aw), cur),
                  pl.BlockSpec((None, WINDOW, kvw), prev), pl.BlockSpec((None, tq, kvw), cur),
                  pl.BlockSpec((None, WINDOW, kvw), prev), pl.BlockSpec((None, tq, kvw), cur)],
        out_specs=pl.BlockSpec((None, tq, aw), cur),
        out_shape=jax.ShapeDtypeStruct((b, l, aw), _BF16),
        compiler_params=_params(2),
        name="attn_prompt",
    )(sinks, q3, k3, k3, v3, v3)


def _cmul(ar, ai, br, bi):
    return ar * br - ai * bi, ar * bi + ai * br


def _discretise(ar, ai, dt):
    mag = jnp.exp(ar * dt)
    ab_re, ab_im = mag * jnp.cos(ai * dt), mag * jnp.sin(ai * dt)
    den = ar * ar + ai * ai
    nr, ni = ab_re - 1.0, ab_im
    f_re = (nr * ar + ni * ai) / den
    f_im = (ni * ar - nr * ai) / den
    return ab_re, ab_im, f_re, f_im


def _pack(re, im):
    return jnp.concatenate([re, im], axis=1)


def _cmul_packed(a, s, s_swapped=None):
    p = a.shape[-1] // 2
    if s_swapped is None:
        s_swapped = pltpu.roll(s, p, axis=1)
    a_rr = jnp.concatenate([a[:, :p], a[:, :p]], axis=1)
    a_ii = jnp.concatenate([-a[:, p:], a[:, p:]], axis=1)
    return s * a_rr + s_swapped * a_ii


def _cpow_bits(base_re, base_im, tau, n_bits):
    out_re = out_im = None
    for bit in range(n_bits):
        on = ((tau >> bit) & 1) == 1
        f_re, f_im = jnp.where(on, base_re, 1.0), jnp.where(on, base_im, 0.0)
        out_re, out_im = (f_re, f_im) if out_re is None else _cmul(out_re, out_im, f_re, f_im)
        if bit + 1 < n_bits:
            base_re, base_im = _cmul(base_re, base_im, base_re, base_im)
    return out_re, out_im


def _ssm_prep_kernel(ldt_ref, a_re_ref, a_im_ref, bt_re_ref, bt_im_ref, c_re_ref, c_im_ref,
                     m_ref, bs_ref, ab_ref, apow_ref, bdb_ref, bdc_ref, kt_ref, *, straight_line):
    t_len, p_len, c_len = CHUNK, SSM_STATE, SSM_GROUP
    dt = jnp.exp(ldt_ref[...])
    ab_re, ab_im, f_re, f_im = _discretise(a_re_ref[...], a_im_ref[...], dt)
    ab = _pack(ab_re, ab_im)
    ab_ref[...] = ab
    bb_re, bb_im = _cmul(f_re, f_im, bt_re_ref[...], bt_im_ref[...])
    bb = _pack(bb_re, bb_im)

    row = lax.broadcasted_iota(jnp.int32, (t_len, 2 * p_len), 0)
    lo = lax.broadcasted_iota(jnp.int32, (t_len, 2 * p_len), 1) < p_len
    n_bits = (t_len - 1).bit_length()
    pw_re, pw_im = _cpow_bits(_pack(ab_re, ab_re), _pack(ab_im, ab_im),
                              jnp.where(lo, row, t_len - 1 - row), n_bits)
    pw_re_sw, pw_im_sw = pltpu.roll(pw_re, p_len, axis=1), pltpu.roll(pw_im, p_len, axis=1)
    w_fwd, w_fwd_sw = jnp.where(lo, pw_re, pw_im_sw), jnp.where(lo, pw_im, pw_re_sw)
    w_rev, w_rev_sw = jnp.where(lo, pw_re_sw, pw_im), jnp.where(lo, pw_im_sw, pw_re)
    w_next = _cmul_packed(ab, w_fwd, w_fwd_sw)
    w_next_sw = _cmul_packed(ab, w_fwd_sw, -w_fwd)
    st_re, st_im = ab_re, ab_im
    for _ in range(n_bits):
        st_re, st_im = _cmul(st_re, st_im, st_re, st_im)
    rows = []
    for _ in range(apow_ref.shape[0]):
        rows.append(_pack(st_re, st_im))
        st_re, st_im = _cmul(st_re, st_im, st_re, st_im)
    apow_ref[...] = jnp.concatenate(rows, axis=0)

    c_re, c_im = c_re_ref[...], c_im_ref[...]
    g_rows = []
    for cp in range(c_len):
        g_re, g_im = _cmul(bb_re[cp:cp + 1, :], bb_im[cp:cp + 1, :], c_re, c_im)
        g_rows.append(_pack(g_re, -g_im))
    g2 = jnp.concatenate(g_rows, axis=0)
    kt_ref[...] = lax.dot_general(g2, w_rev, (((1,), (1,)), ((), ())), preferred_element_type=_F32,
                                  precision=lax.Precision.HIGHEST)

    t_idx = lax.broadcasted_iota(jnp.int32, (t_len, t_len), 0)
    s_idx = lax.broadcasted_iota(jnp.int32, (t_len, t_len), 1)
    causal = s_idx <= t_idx

    def toeplitz_rows(c, carry):
        for cp in range(c_len):
            taps = kt_ref[pl.ds(cp * c_len + c, 1), :]
            full = jnp.broadcast_to(taps, (t_len, t_len))
            skew = pltpu.roll(full, 1, axis=1, stride=1, stride_axis=0)
            m_ref[pl.ds(pl.multiple_of(c * t_len, t_len), t_len), cp * t_len:(cp + 1) * t_len] = (
                jnp.where(causal, skew, 0.0).astype(m_ref.dtype))
        return carry

    if straight_line:
        for c in range(c_len):
            toeplitz_rows(c, 0)
    else:
        lax.fori_loop(0, c_len, toeplitz_rows, 0, unroll=2)

    for cp in range(c_len):
        bs_ref[cp * t_len:(cp + 1) * t_len, :] = _cmul_packed(bb[cp:cp + 1, :], w_rev, w_rev_sw).astype(bs_ref.dtype)

    lane = lax.broadcasted_iota(jnp.int32, (1, 2 * p_len), 1)
    conj = jnp.where(lane < p_len, 1.0, -1.0)
    c_pk = _pack(c_re, c_im)
    for c in range(c_len):
        m_ref[c * t_len:(c + 1) * t_len, c_len * t_len:] = (
            _cmul_packed(c_pk[c:c + 1, :], w_next, w_next_sw) * conj).astype(m_ref.dtype)

    n_state = bdb_ref.shape[1] // 2
    mine = lax.broadcasted_iota(jnp.int32, (c_len, n_state), 1) // p_len == pl.program_id(0)

    def spread(block):
        return jnp.where(mine, jnp.concatenate([block] * (n_state // p_len), axis=1), 0.0)

    bdb_ref[...] = _pack(spread(bb_re), spread(bb_im)).astype(bdb_ref.dtype)
    bdc_ref[...] = _pack(spread(c_re), spread(c_im)).astype(bdc_ref.dtype)


def _ssm_prep_parts(a_re, a_im, log_dt, b_re, b_im, c_re, c_im, n_pow, *, straight_line):
    g, p = a_re.shape
    c = b_re.shape[2]
    tw = c * CHUNK
    grp = lambda *shape: pl.BlockSpec((None,) + shape, lambda i: (i,) + (0,) * len(shape))
    rows = pl.BlockSpec((c, 2 * g * p), lambda i: (i, 0))
    return dict(
        kernel=functools.partial(_ssm_prep_kernel, straight_line=straight_line),
        operands=[log_dt.reshape(g, 1, 1), a_re.reshape(g, 1, p), a_im.reshape(g, 1, p),
                  jnp.swapaxes(b_re, 1, 2), jnp.swapaxes(b_im, 1, 2), c_re, c_im],
        in_specs=[grp(1, 1), grp(1, p), grp(1, p), grp(c, p), grp(c, p), grp(c, p), grp(c, p)],
        out_specs=[grp(tw, tw + 2 * p), grp(tw, 2 * p), grp(1, 2 * p), grp(n_pow, 2 * p), rows, rows],
        out_shape=[jax.ShapeDtypeStruct((g, tw, tw + 2 * p), _BF16),
                   jax.ShapeDtypeStruct((g, tw, 2 * p), _BF16),
                   jax.ShapeDtypeStruct((g, 1, 2 * p), _F32),
                   jax.ShapeDtypeStruct((g, n_pow, 2 * p), _F32),
                   jax.ShapeDtypeStruct((g * c, 2 * g * p), _BF16),
                   jax.ShapeDtypeStruct((g * c, 2 * g * p), _BF16)],
        scratch_shapes=[pltpu.VMEM((c * c, CHUNK), _F32)])


def _ssm_prompt_kernel(d_ref, u_ref, mt_ref, bs_ref, apow_ref, g_ref, st_ref, *, chunks_per_seq, row_block):
    n_grp = mt_ref.shape[0]
    c_len = u_ref.shape[0] // n_grp
    n_tok = u_ref.shape[1]
    nc = n_tok // CHUNK
    for gi in range(n_grp):
        grp = pl.program_id(0) * n_grp + gi
        u3 = u_ref[gi * c_len:(gi + 1) * c_len, :].reshape(c_len, nc, CHUNK)
        x = jnp.concatenate([u3[c] for c in range(c_len)], axis=1)
        s = jnp.dot(x, bs_ref[gi], preferred_element_type=_F32)
        k_idx = lax.broadcasted_iota(jnp.int32, s.shape, 0) % chunks_per_seq
        for j in range(apow_ref.shape[1]):
            d = 1 << j
            shifted = jnp.where(k_idx >= d, pltpu.roll(s, d, axis=0), 0.0)
            s = s + _cmul_packed(apow_ref[gi, j:j + 1, :], shifted)
        st_ref[gi] = s
        s_prev = jnp.where(k_idx >= 1, pltpu.roll(s, 1, axis=0), 0.0)
        xs_t = jnp.concatenate([u3[c].astype(_F32).T for c in range(c_len)] + [s_prev.T], axis=0).astype(_BF16)
        g_slabs = []
        for rb in range(c_len * CHUNK // row_block):
            y_t = jnp.dot(mt_ref[gi, rb * row_block:(rb + 1) * row_block, :], xs_t, preferred_element_type=_F32)
            for ci in range(row_block // CHUNK):
                c = rb * (row_block // CHUNK) + ci
                yc = y_t[ci * CHUNK:(ci + 1) * CHUNK, :].T + d_ref[grp * c_len + c] * u3[c].astype(_F32)
                g_slabs.append(jax.nn.gelu(yc).astype(g_ref.dtype))
        g_ref[gi * c_len:(gi + 1) * c_len, :] = jnp.stack(g_slabs, axis=0).reshape(c_len, n_tok)


def _ssm_prompt(u_t, d_skip, m, bs, apow, chunks_per_seq, *, groups_per_step):
    width, n = u_t.shape
    g, tw, kw = m.shape
    c = width // g
    nc = n // CHUNK
    p2 = bs.shape[2]
    n_pow = apow.shape[1]
    gps = groups_per_step
    grp = lambda *shape: pl.BlockSpec((gps,) + shape, lambda i: (i,) + (0,) * len(shape))
    return pl.pallas_call(
        functools.partial(_ssm_prompt_kernel, chunks_per_seq=chunks_per_seq, row_block=min(tw, 512)),
        grid=(g // gps,),
        in_specs=[pl.BlockSpec(memory_space=pltpu.SMEM),
                  pl.BlockSpec((gps * c, n), lambda i: (i, 0)),
                  grp(tw, kw), grp(tw, p2), grp(n_pow, p2)],
        out_specs=[pl.BlockSpec((gps * c, n), lambda i: (i, 0)), grp(nc, p2)],
        out_shape=[jax.ShapeDtypeStruct((width, n), _BF16),
                   jax.ShapeDtypeStruct((g, nc, p2), _F32)],
        compiler_params=_params(),
        name="ssm_prompt",
    )(d_skip, u_t, m, bs, apow)


def _mix_mlp_kernel(x_ref, att_ref, g_ref, wglu_ref, bglu_ref, wout_ref, gpost_ref, gpre_ref,
                    wup_ref, wdown_ref, gmlp_ref, o_ref, *, g_transposed, ff_chunk, sub_rows):
    aw = att_ref.shape[1]
    d_ff = wup_ref.shape[1]

    def mix_stage(rows):
        if g_transposed:
            gact = g_ref[:, rows]
            z = jnp.dot(wglu_ref[...], gact, preferred_element_type=_F32) + bglu_ref[...]
        else:
            gact = g_ref[rows, :]
            z = jnp.dot(gact.astype(_BF16), wglu_ref[...], preferred_element_type=_F32) + bglu_ref[...]
        ssm = (gact.astype(_F32) * (1.0 / (1.0 + jnp.exp(-z)))).astype(_BF16)
        mix = jnp.dot(att_ref[rows, :], wout_ref[:aw, :], preferred_element_type=_F32)
        if g_transposed:
            mix = mix + lax.dot_general(ssm, wout_ref[aw:, :], (((0,), (0,)), ((), ())),
                                        preferred_element_type=_F32)
        else:
            mix = mix + jnp.dot(ssm, wout_ref[aw:, :], preferred_element_type=_F32)
        x1 = x_ref[rows, :] + _rms(mix, gpost_ref[...])
        return x1, _rms(x1, gpre_ref[...]).astype(_BF16)

    def mlp_stage(rows, x1, h):
        f = None
        for j in range(d_ff // ff_chunk):
            up = jnp.dot(h, wup_ref[:, j * ff_chunk:(j + 1) * ff_chunk], preferred_element_type=_F32)
            act = jnp.square(jnp.maximum(up, 0.0)).astype(_BF16)
            part = jnp.dot(act, wdown_ref[j * ff_chunk:(j + 1) * ff_chunk, :], preferred_element_type=_F32)
            f = part if f is None else f + part
        o_ref[rows, :] = x1 + _rms(f, gmlp_ref[...])

    n_sub = x_ref.shape[0] // sub_rows
    row_slices = [slice(s * sub_rows, (s + 1) * sub_rows) for s in range(n_sub)]
    staged = mix_stage(row_slices[0])
    for s in range(n_sub):
        nxt = mix_stage(row_slices[s + 1]) if s + 1 < n_sub else None
        mlp_stage(row_slices[s], *staged)
        staged = nxt


def _mix_mlp(x2, att, gact, wglu, bglu, wout, gpost, gpre, wup, wdown, gmlp, *, tm, g_transposed):
    n, d = x2.shape
    aw = att.shape[1]
    sw = wglu.shape[0]
    d_ff = wup.shape[1]
    if g_transposed:
        g_spec = pl.BlockSpec((sw, tm), lambda i: (0, i))
        bglu2 = bglu.reshape(sw, 1)
    else:
        g_spec = pl.BlockSpec((tm, sw), lambda i: (i, 0))
        bglu2 = bglu.reshape(1, sw)
    return pl.pallas_call(
        functools.partial(_mix_mlp_kernel, g_transposed=g_transposed, ff_chunk=min(d_ff, 1024),
                          sub_rows=min(tm, 512)),
        grid=(n // tm,),
        in_specs=[pl.BlockSpec((tm, d), lambda i: (i, 0)), pl.BlockSpec((tm, aw), lambda i: (i, 0)), g_spec,
                  _const_spec(wglu.shape), _const_spec(bglu2.shape), _const_spec(wout.shape),
                  _const_spec((1, d)), _const_spec((1, d)), _const_spec(wup.shape),
                  _const_spec(wdown.shape), _const_spec((1, d))],
        out_specs=pl.BlockSpec((tm, d), lambda i: (i, 0)),
        out_shape=jax.ShapeDtypeStruct((n, d), _F32),
        compiler_params=_params(),
        name="mix_mlp_t" if g_transposed else "mix_mlp_n",
    )(x2, att, gact, wglu, bglu2, wout, gpost.reshape(1, d), gpre.reshape(1, d), wup, wdown,
      gmlp.reshape(1, d))


def _attn_sample_kernel(sink_ref, q_ref, kn_ref, vn_ref, kc_ref, vc_ref, o_ref, *, n_heads):
    steps, bb, _ = q_ref.shape
    win = kc_ref.shape[3]
    group = n_heads // N_KV_HEADS
    rows = group * steps * bb
    r_c = lax.broadcasted_iota(jnp.int32, (rows, bb * win), 0)
    c_c = lax.broadcasted_iota(jnp.int32, (rows, bb * win), 1)
    t_r, b_r = (r_c % (steps * bb)) // bb, r_c % bb
    mask_c = (c_c // win == b_r) & (c_c % win > t_r + win - WINDOW)
    r_n = lax.broadcasted_iota(jnp.int32, (rows, steps * bb), 0)
    c_n = lax.broadcasted_iota(jnp.int32, (rows, steps * bb), 1)
    t_rn, b_rn = (r_n % (steps * bb)) // bb, r_n % bb
    dt_n = t_rn - c_n // bb
    mask_n = (c_n % bb == b_rn) & (dt_n >= 0) & (dt_n < WINDOW)
    g_row = lax.broadcasted_iota(jnp.int32, (rows, 1), 0) // (steps * bb)
    scale = HEAD_DIM ** -0.5
    nt = (((1,), (1,)), ((), ()))
    q = q_ref[...].reshape(steps * bb, n_heads * HEAD_DIM)
    kn = kn_ref[...].reshape(steps * bb, N_KV_HEADS * HEAD_DIM)
    vn = vn_ref[...].reshape(steps * bb, N_KV_HEADS * HEAD_DIM)
    outs = []
    for kv in range(N_KV_HEADS):
        ls = slice(kv * HEAD_DIM, (kv + 1) * HEAD_DIM)
        qs = jnp.concatenate([q[:, (kv * group + gi) * HEAD_DIM:(kv * group + gi + 1) * HEAD_DIM]
                              for gi in range(group)], axis=0).astype(_BF16)
        sink = jnp.zeros((rows, 1), _F32)
        for gi in range(group):
            sink = jnp.where(g_row == gi, sink_ref[kv * group + gi], sink)
        kc_t = jnp.concatenate([kc_ref[b, kv] for b in range(bb)], axis=1).astype(_BF16)
        vc_t = jnp.concatenate([vc_ref[b, kv] for b in range(bb)], axis=1).astype(_BF16)
        s_c = jnp.dot(qs, kc_t, preferred_element_type=_F32) * scale
        s_n = lax.dot_general(qs, kn[:, ls].astype(_BF16), nt, preferred_element_type=_F32) * scale
        s_c = jnp.where(mask_c, s_c, NEG_INF)
        s_n = jnp.where(mask_n, s_n, NEG_INF)
        p_c, p_n = _softmax_sink([s_c, s_n], sink)
        o = (lax.dot_general(p_c.astype(_BF16), vc_t, nt, preferred_element_type=_F32)
             + jnp.dot(p_n.astype(_BF16), vn[:, ls].astype(_BF16), preferred_element_type=_F32))
        outs.extend(o[gi * steps * bb:(gi + 1) * steps * bb, :] for gi in range(group))
    o_ref[...] = jnp.concatenate(outs, axis=1).reshape(o_ref.shape).astype(o_ref.dtype)


def _attn_sample(q3, kn3, vn3, kc_t, vc_t, sinks, *, bb):
    steps, nb, aw = q3.shape
    kvw = kn3.shape[2]
    tok = lambda i: (0, i, 0)
    cache_spec = pl.BlockSpec((bb,) + kc_t.shape[1:], lambda i: (i, 0, 0, 0))
    return pl.pallas_call(
        functools.partial(_attn_sample_kernel, n_heads=aw // HEAD_DIM),
        grid=(nb // bb,),
        in_specs=[pl.BlockSpec(memory_space=pltpu.SMEM),
                  pl.BlockSpec((steps, bb, aw), tok), pl.BlockSpec((steps, bb, kvw), tok),
                  pl.BlockSpec((steps, bb, kvw), tok), cache_spec, cache_spec],
        out_specs=pl.BlockSpec((steps, bb, aw), tok),
        out_shape=jax.ShapeDtypeStruct((steps, nb, aw), _BF16),
        compiler_params=_params(),
        name="attn_sample",
    )(sinks, q3, kn3, vn3, kc_t, vc_t)


def _ssm_sample_kernel(u_ref, s0r_ref, s0i_ref, ab_ref, bdb_ref, bdc_ref, d_ref,
                       g_ref, sr_ref, si_ref, *, steps):
    nb = s0r_ref.shape[0]
    ns = s0r_ref.shape[1]
    nt = (((1,), (1,)), ((), ()))
    u = u_ref[...]
    bu = jnp.dot(u.astype(_BF16), bdb_ref[...], preferred_element_type=_F32)
    a_re, a_im = ab_ref[0:1, :], ab_ref[1:2, :]
    xr, xi = s0r_ref[...], s0i_ref[...]
    for t in range(steps):
        rs = slice(t * nb, (t + 1) * nb)
        nr = a_re * xr - a_im * xi + bu[rs, :ns]
        ni = a_re * xi + a_im * xr + bu[rs, ns:]
        xr, xi = nr, ni
        y = (lax.dot_general(xr.astype(_BF16), bdc_ref[:, :ns], nt, preferred_element_type=_F32)
             - lax.dot_general(xi.astype(_BF16), bdc_ref[:, ns:], nt, preferred_element_type=_F32))
        g_ref[rs, :] = jax.nn.gelu(y + d_ref[...] * u[rs, :])
    sr_ref[...] = xr
    si_ref[...] = xi


def _ssm_sample(u2, s0r, s0i, ab2, bd_b, bd_c, d_skip, *, steps):
    n, sw = u2.shape
    nb, ns = s0r.shape
    return pl.pallas_call(
        functools.partial(_ssm_sample_kernel, steps=steps),
        grid=(1,),
        in_specs=[_const_spec(u2.shape), _const_spec(s0r.shape), _const_spec(s0i.shape),
                  _const_spec(ab2.shape), _const_spec(bd_b.shape), _const_spec(bd_c.shape),
                  _const_spec((1, sw))],
        out_specs=[pl.BlockSpec((n, sw), lambda i: (0, 0)), pl.BlockSpec((nb, ns), lambda i: (0, 0)),
                   pl.BlockSpec((nb, ns), lambda i: (0, 0))],
        out_shape=[jax.ShapeDtypeStruct((n, sw), _F32), jax.ShapeDtypeStruct((nb, ns), _F32),
                   jax.ShapeDtypeStruct((nb, ns), _F32)],
        compiler_params=_params(),
        name="ssm_sample",
    )(u2, s0r, s0i, ab2, bd_b, bd_c, d_skip.reshape(1, sw))


def _tile_rows(n, target):
    tm = min(n, target)
    while n % tm:
        tm //= 2
    return tm


def _prompt_layer(x, w, ssm_weights, n_pow):
    b, l, d = x.shape
    n = b * l
    x2 = x.reshape(n, d)
    n_grp = ssm_weights[0].shape[0]
    tables = _rotary_tables(np.arange(l))
    tm = n // n_grp
    fuse = n % n_grp == 0 and tm % LANES == 0 and l % tm == 0 and tm <= 1024
    if not fuse:
        tm = _tile_rows(l, 1024)
    inproj = _inproj_parts(x2, w["g_mix_pre"], w["wqkv"], w["wu_t"], tables, tm=tm,
                           u_transposed=True, q_dtype=_BF16, u_dtype=_BF16)
    prep_parts = _ssm_prep_parts(*ssm_weights, n_pow, straight_line=fuse)
    if fuse:
        (q, k, v, u_t), prep = _call_fused(inproj, prep_parts, (n_grp,), "inproj_prep")
    else:
        q, k, v, u_t = _call_parts(inproj, (n // tm,), "inproj_t")
        prep = _call_parts(prep_parts, (n_grp,), "ssm_prep")
    aw, kvw = q.shape[1], k.shape[1]
    att = _attn_prompt(q.reshape(b, l, aw), k.reshape(b, l, kvw), v.reshape(b, l, kvw), w["sinks"],
                       blocks_per_step=_tile_rows(l // WINDOW, 4))
    sw = u_t.shape[0]
    kc = l // CHUNK
    m, bs, _, apow, _, _ = prep
    g_t, states = _ssm_prompt(u_t, w["d_skip"], m, bs, apow, kc, groups_per_step=_tile_rows(m.shape[0], 2))
    y = _mix_mlp(x2, att.reshape(n, aw), g_t, w["wglu_t"], w["b_glu"], w["wout"],
                 w["g_mix_post"], w["g_mlp_pre"], w["wup"], w["wdown"], w["g_mlp_post"],
                 tm=_tile_rows(n, 1024), g_transposed=True)
    keep = min(WINDOW, l)
    k_state = k.reshape(b, l, kvw)[:, l - keep:].reshape(b, keep, N_KV_HEADS, HEAD_DIM)
    v_state = v.reshape(b, l, kvw)[:, l - keep:].reshape(b, keep, N_KV_HEADS, HEAD_DIM)
    fin = states[:, kc - 1::kc, :]
    s_re = jnp.swapaxes(fin[:, :, :SSM_STATE], 0, 1)
    s_im = jnp.swapaxes(fin[:, :, SSM_STATE:], 0, 1)
    return (y.reshape(b, l, d), k_state, v_state, s_re, s_im), prep


def _sample_layer(x, cache_k, cache_v, s0_re, s0_im, w, prep):
    nb, steps, d = x.shape
    n = nb * steps
    xs = jnp.swapaxes(x, 0, 1).reshape(n, d)
    tables = tuple(np.repeat(t, nb, axis=0) for t in _rotary_tables(PAST_LEN + np.arange(steps)))
    q, k, v, u = _call_parts(_inproj_parts(xs, w["g_mix_pre"], w["wqkv"], w["wu"], tables, tm=n,
                                           u_transposed=False, q_dtype=_F32, u_dtype=_F32), (1,), "inproj_n")
    aw, kvw = q.shape[1], k.shape[1]
    win = cache_k.shape[1]
    att = _attn_sample(q.reshape(steps, nb, aw), k.reshape(steps, nb, kvw), v.reshape(steps, nb, kvw),
                       jnp.transpose(cache_k, (0, 2, 3, 1)), jnp.transpose(cache_v, (0, 2, 3, 1)),
                       w["sinks"], bb=_tile_rows(nb, 8))
    _, _, ab, _, bd_b, bd_c = prep
    p = SSM_STATE
    n_grp = ab.shape[0]
    ab2 = jnp.stack([ab[:, 0, :p].reshape(-1), ab[:, 0, p:].reshape(-1)])
    gact, s_re, s_im = _ssm_sample(u, s0_re.reshape(nb, n_grp * p), s0_im.reshape(nb, n_grp * p),
                                   ab2, bd_b, bd_c, w["d_skip"], steps=steps)
    y = _mix_mlp(xs, att.reshape(n, aw), gact, w["wglu"], w["b_glu"], w["wout"],
                 w["g_mix_post"], w["g_mlp_pre"], w["wup"], w["wdown"], w["g_mlp_post"],
                 tm=_tile_rows(n, 512), g_transposed=False)
    y = jnp.swapaxes(y.reshape(steps, nb, d), 0, 1)
    k_new = jnp.swapaxes(k.reshape(steps, nb, N_KV_HEADS, HEAD_DIM), 0, 1)
    v_new = jnp.swapaxes(v.reshape(steps, nb, N_KV_HEADS, HEAD_DIM), 0, 1)
    k_state = jnp.concatenate([cache_k, k_new], axis=1)[:, -win:]
    v_state = jnp.concatenate([cache_v, v_new], axis=1)[:, -win:]
    return y, k_state, v_state, s_re.reshape(nb, n_grp, p), s_im.reshape(nb, n_grp, p)


def kernel(x_prompt, x_sample, cache_k, cache_v, state_ssm_re, state_ssm_im, g_mix_pre, w_in, sinks,
           a_re, a_im, log_dt, b_re, b_im, c_re, c_im, d_skip, w_glu, b_glu, w_out, g_mix_post,
           g_mlp_pre, w_up, w_down, g_mlp_post):
    depth = w_in.shape[0]
    seq = x_prompt.shape[1]
    assert seq % CHUNK == 0 and seq % WINDOW == 0
    chunks_per_seq = seq // CHUNK
    n_pow = max(1, (chunks_per_seq - 1).bit_length())
    ssm_width = d_skip.shape[1]
    qkv_width = w_in.shape[2] - ssm_width
    yp, ys = x_prompt, x_sample
    outs = [[] for _ in range(8)]
    for l in range(depth):
        w = {
            "g_mix_pre": g_mix_pre[l], "sinks": sinks[l], "d_skip": d_skip[l], "b_glu": b_glu[l],
            "g_mix_post": g_mix_post[l], "g_mlp_pre": g_mlp_pre[l], "g_mlp_post": g_mlp_post[l],
            "c_re": c_re[l], "c_im": c_im[l],
            "wqkv": w_in[l, :, :qkv_width].astype(_BF16),
            "wu": w_in[l, :, qkv_width:].astype(_BF16),
            "wu_t": w_in[l, :, qkv_width:].T.astype(_BF16),
            "wglu": w_glu[l].astype(_BF16), "wglu_t": w_glu[l].T.astype(_BF16),
            "wout": w_out[l].astype(_BF16), "wup": w_up[l].astype(_BF16), "wdown": w_down[l].astype(_BF16),
        }
        (yp, kp, vp, srp, sip), prep = _prompt_layer(
            yp, w, (a_re[l], a_im[l], log_dt[l], b_re[l], b_im[l], c_re[l], c_im[l]), n_pow)
        ys, kss, vss, srs, sis = _sample_layer(ys, cache_k[l], cache_v[l], state_ssm_re[l],
                                               state_ssm_im[l], w, prep)
        for lst, val in zip(outs, (kp, vp, srp, sip, kss, vss, srs, sis)):
            lst.append(val)
    return (yp, ys) + tuple(jnp.stack(o) for o in outs)
```

```python
import functools
import math

import jax
import jax.numpy as jnp
import numpy as np
from jax import lax
from jax.experimental import pallas as pl
from jax.experimental.pallas import tpu as pltpu

HEAD_DIM = 64
N_KV_HEADS = 2
WINDOW = 128
ROT_DIM = HEAD_DIM // 4
ROPE_THETA = 500000.0
SSM_GROUP = 16
SSM_STATE = 64
NORM_EPS = 1e-6
NEG_INF = -1e30
PAST_LEN = 16384
CHUNK = 128
LANES = 128
VMEM_LIMIT = 60 * 1024 * 1024

_F32 = jnp.float32
_BF16 = jnp.bfloat16


def _params(n_parallel=1):
    return pltpu.CompilerParams(dimension_semantics=("arbitrary",) * n_parallel,
                                vmem_limit_bytes=VMEM_LIMIT)


def _rms(x, g):
    return x * lax.rsqrt(jnp.mean(x * x, axis=-1, keepdims=True) + NORM_EPS) * g


def _const_spec(shape):
    nd = len(shape)
    return pl.BlockSpec(shape, lambda *_: (0,) * nd, pipeline_mode=pl.Buffered(1))


def _rotary_tables(pos):
    half = ROT_DIM // 2
    inv_freq = np.power(ROPE_THETA, -np.arange(half, dtype=np.float64) * (2.0 / ROT_DIM))
    ang = pos.astype(np.float64)[:, None] * inv_freq[None, :]
    cos, sin = np.cos(ang), np.sin(ang)
    n = pos.shape[0]
    ones = np.ones((n, HEAD_DIM - ROT_DIM))
    zeros_h = np.zeros((n, half))
    zeros_r = np.zeros((n, HEAD_DIM - ROT_DIM))
    c_tab = np.concatenate([cos, cos, ones], axis=1)
    s_up = np.concatenate([-sin, zeros_h, zeros_r], axis=1)
    s_dn = np.concatenate([zeros_h, sin, zeros_r], axis=1)
    rep = LANES // HEAD_DIM
    return tuple(np.tile(t, (1, rep)).astype(np.float32) for t in (c_tab, s_up, s_dn))


def _inproj_kernel(x_ref, g_ref, wqkv_ref, wu_ref, c_ref, su_ref, sd_ref,
                   q_ref, k_ref, v_ref, u_ref, *, attn_width, kv_width, u_transposed):
    half = ROT_DIM // 2
    h = _rms(x_ref[...], g_ref[...]).astype(_BF16)
    qkv = jnp.dot(h, wqkv_ref[...], preferred_element_type=_F32)
    c_tab, s_up, s_dn = c_ref[...], su_ref[...], sd_ref[...]

    def rot(blk):
        return (blk * c_tab + pltpu.roll(blk, LANES - half, axis=1) * s_up
                + pltpu.roll(blk, half, axis=1) * s_dn)

    for j in range(attn_width // LANES):
        q_ref[:, j * LANES:(j + 1) * LANES] = rot(qkv[:, j * LANES:(j + 1) * LANES]).astype(q_ref.dtype)
    for j in range(kv_width // LANES):
        o = attn_width + j * LANES
        k_ref[:, j * LANES:(j + 1) * LANES] = rot(qkv[:, o:o + LANES])
    v_ref[...] = qkv[:, attn_width + kv_width:]
    if u_transposed:
        u = lax.dot_general(wu_ref[...], h, (((1,), (1,)), ((), ())), preferred_element_type=_F32)
    else:
        u = jnp.dot(h, wu_ref[...], preferred_element_type=_F32)
    u_ref[...] = u.astype(u_ref.dtype)


def _inproj_parts(x2, g, wqkv, wu, tables, *, tm, u_transposed, q_dtype, u_dtype):
    n, d = x2.shape
    kv_width = N_KV_HEADS * HEAD_DIM
    attn_width = wqkv.shape[1] - 2 * kv_width
    ssm_width = wu.shape[0] if u_transposed else wu.shape[1]
    tab_blocks = tables[0].shape[0] // tm
    tab_spec = pl.BlockSpec((tm, LANES), lambda i: (i % tab_blocks, 0))
    if u_transposed:
        u_shape, u_spec = (ssm_width, n), pl.BlockSpec((ssm_width, tm), lambda i: (0, i))
    else:
        u_shape, u_spec = (n, ssm_width), pl.BlockSpec((tm, ssm_width), lambda i: (i, 0))
    return dict(
        kernel=functools.partial(_inproj_kernel, attn_width=attn_width, kv_width=kv_width,
                                 u_transposed=u_transposed),
        operands=[x2, g.reshape(1, d), wqkv, wu, *tables],
        in_specs=[pl.BlockSpec((tm, d), lambda i: (i, 0)), _const_spec((1, d)),
                  _const_spec(wqkv.shape), _const_spec(wu.shape), tab_spec, tab_spec, tab_spec],
        out_specs=[pl.BlockSpec((tm, attn_width), lambda i: (i, 0)),
                   pl.BlockSpec((tm, kv_width), lambda i: (i, 0)),
                   pl.BlockSpec((tm, kv_width), lambda i: (i, 0)), u_spec],
        out_shape=[jax.ShapeDtypeStruct((n, attn_width), q_dtype),
                   jax.ShapeDtypeStruct((n, kv_width), _F32),
                   jax.ShapeDtypeStruct((n, kv_width), _F32),
                   jax.ShapeDtypeStruct(u_shape, u_dtype)],
        scratch_shapes=[])


def _call_parts(parts, grid, name):
    return pl.pallas_call(parts["kernel"], grid=grid, in_specs=parts["in_specs"], out_specs=parts["out_specs"],
                          out_shape=parts["out_shape"], scratch_shapes=parts["scratch_shapes"],
                          compiler_params=_params(len(grid)), name=name)(*parts["operands"])


def _fused_kernel(*refs, first, second):
    a_in, a_out, a_scr = (len(first[k]) for k in ("in_specs", "out_specs", "scratch_shapes"))
    n_in = a_in + len(second["in_specs"])
    n_out = a_out + len(second["out_specs"])
    ins, outs, scr = refs[:n_in], refs[n_in:n_in + n_out], refs[n_in + n_out:]
    first["kernel"](*ins[:a_in], *outs[:a_out], *scr[:a_scr])
    second["kernel"](*ins[a_in:], *outs[a_out:], *scr[a_scr:])


def _call_fused(first, second, grid, name):
    outs = pl.pallas_call(
        functools.partial(_fused_kernel, first=first, second=second), grid=grid,
        in_specs=first["in_specs"] + second["in_specs"], out_specs=first["out_specs"] + second["out_specs"],
        out_shape=first["out_shape"] + second["out_shape"],
        scratch_shapes=first["scratch_shapes"] + second["scratch_shapes"],
        compiler_params=_params(len(grid)), name=name)(*first["operands"], *second["operands"])
    n_first = len(first["out_shape"])
    return outs[:n_first], outs[n_first:]


def _softmax_sink(s_parts, sink):
    m = sink
    for s in s_parts:
        m = jnp.maximum(m, jnp.max(s, axis=-1, keepdims=True))
    p_parts = [jnp.exp(s - m) for s in s_parts]
    denom = jnp.exp(sink - m)
    for p in p_parts:
        denom = denom + jnp.sum(p, axis=-1, keepdims=True)
    return [p / denom for p in p_parts]


def _attn_prompt_kernel(sink_ref, q_ref, kp_ref, kc_ref, vp_ref, vc_ref, o_ref, *, n_heads, first_step):
    blk = kp_ref.shape[0]
    n_sub = q_ref.shape[0] // blk
    heads_per_lane_block = LANES // HEAD_DIM
    kv_lane_blocks = n_heads // N_KV_HEADS // heads_per_lane_block
    row = lax.broadcasted_iota(jnp.int32, (blk, blk), 0)
    col = lax.broadcasted_iota(jnp.int32, (blk, blk), 1)
    own = col <= row
    lane = lax.broadcasted_iota(jnp.int32, (1, LANES), 1)
    scale = HEAD_DIM ** -0.5
    nt = (((1,), (1,)), ((), ()))
    k_all = jnp.concatenate([kp_ref[...], kc_ref[...]], axis=0)
    v_all = jnp.concatenate([vp_ref[...], vc_ref[...]], axis=0).astype(_BF16)
    k_swap = pltpu.roll(k_all, HEAD_DIM, axis=1)
    k_dup = [jnp.where((lane < HEAD_DIM) == (kv == 0), k_all, k_swap).astype(_BF16) for kv in range(N_KV_HEADS)]
    def do_block(sub):
        rows = slice(sub * blk, (sub + 1) * blk)
        keys = slice(sub * blk, (sub + 2) * blk)
        valid = (own | jnp.logical_not(first_step)) if sub == 0 else None
        for kv in range(N_KV_HEADS):
            p_rows, inv_denoms = [], []
            q_rows = []
            for pr in range(kv_lane_blocks):
                j = kv * kv_lane_blocks + pr
                qp = q_ref[rows, j * LANES:(j + 1) * LANES] * scale
                q_rows.extend(jnp.where(lane // HEAD_DIM == e, qp, 0.0).astype(qp.dtype)
                              for e in range(heads_per_lane_block))
            s_kv = lax.dot_general(jnp.concatenate(q_rows, axis=0), k_dup[kv][keys], nt,
                                   preferred_element_type=_F32)
            for hi in range(len(q_rows)):
                sink = sink_ref[kv * len(q_rows) + hi]
                s_prev = s_kv[hi * blk:(hi + 1) * blk, :blk]
                s_own = s_kv[hi * blk:(hi + 1) * blk, blk:]
                s = jnp.where(own, s_own, s_prev)
                if valid is not None:
                    s = jnp.where(valid, s, NEG_INF)
                m = jnp.maximum(jnp.max(s, axis=-1, keepdims=True), sink)
                p = jnp.exp(s - m)
                denom = jnp.sum(p, axis=-1, keepdims=True) + jnp.exp(sink - m)
                p_rows.append(jnp.concatenate([jnp.where(own, 0.0, p), jnp.where(own, p, 0.0)],
                                              axis=1).astype(_BF16))
                inv_denoms.append(1.0 / denom)
            o_kv = jnp.dot(jnp.concatenate(p_rows, axis=0), v_all[keys], preferred_element_type=_F32)
            for pr in range(kv_lane_blocks):
                parts = []
                for e in range(heads_per_lane_block):
                    hi = pr * heads_per_lane_block + e
                    o_h = o_kv[hi * blk:(hi + 1) * blk, :] * inv_denoms[hi]
                    shift = (e - kv) * HEAD_DIM % LANES
                    parts.append(pltpu.roll(o_h, shift, axis=1) if shift else o_h)
                out = parts[0]
                for e in range(1, heads_per_lane_block):
                    out = jnp.where(lane >= e * HEAD_DIM, parts[e], out)
                j = kv * kv_lane_blocks + pr
                o_ref[rows, j * LANES:(j + 1) * LANES] = out.astype(o_ref.dtype)

    return [functools.partial(do_block, sub) for sub in range(n_sub)]


def _attn_mix_kernel(sink_ref, q_ref, kp_ref, kc_ref, vp_ref, vc_ref,
                     x_ref, g_ref, wglu_ref, bglu_ref, wout_ref, gpost_ref, gpre_ref, wup_ref, wdown_ref, gmlp_ref,
                     o_ref, att_ref, *, n_heads, n_tiles, tiles_per_seq, ff_chunk, sub_rows):
    step = pl.program_id(0)

    @pl.when(step == 0)
    def _():
        att_ref[...] = jnp.zeros_like(att_ref)

    tile = jnp.minimum(step, n_tiles - 1)
    attn_blocks = _attn_prompt_kernel(sink_ref, q_ref, kp_ref, kc_ref, vp_ref, vc_ref, att_ref, n_heads=n_heads,
                                      first_step=tile % tiles_per_seq == 0)
    assert x_ref.shape[0] <= 2 * sub_rows
    _mix_mlp_kernel(x_ref, att_ref, g_ref, wglu_ref, bglu_ref, wout_ref, gpost_ref, gpre_ref,
                    wup_ref, wdown_ref, gmlp_ref, o_ref, g_transposed=True, ff_chunk=ff_chunk, sub_rows=sub_rows,
                    filler=attn_blocks)


def _attn_mix_mlp(q2, k2, v2, sinks, x2, g_t, wglu_t, bglu, wout, gpost, gpre, wup, wdown, gmlp, *, tm, seq):
    n, d = x2.shape
    aw, kvw = q2.shape[1], k2.shape[1]
    sw = wglu_t.shape[0]
    d_ff = wup.shape[1]
    n_tiles = n // tm
    blocks = tm // WINDOW
    att_tile = lambda s: jnp.minimum(s, n_tiles - 1)
    mlp_tile = lambda s: jnp.maximum(s - 1, 0)
    cur = lambda s: (att_tile(s), 0)
    prev = lambda s: (jnp.maximum(att_tile(s) * blocks - 1, 0), 0)
    return pl.pallas_call(
        functools.partial(_attn_mix_kernel, n_heads=aw // HEAD_DIM, n_tiles=n_tiles, tiles_per_seq=seq // tm,
                          ff_chunk=min(d_ff, 1024), sub_rows=min(tm, 512)),
        grid=(n_tiles + 1,),
        in_specs=[pl.BlockSpec(memory_space=pltpu.SMEM),
                  pl.BlockSpec((tm, aw), cur),
                  pl.BlockSpec((WINDOW, kvw), prev), pl.BlockSpec((tm, kvw), cur),
                  pl.BlockSpec((WINDOW, kvw), prev), pl.BlockSpec((tm, kvw), cur),
                  pl.BlockSpec((tm, d), lambda s: (mlp_tile(s), 0)),
                  pl.BlockSpec((sw, tm), lambda s: (0, mlp_tile(s))),
                  _const_spec(wglu_t.shape), _const_spec((sw, 1)), _const_spec(wout.shape),
                  _const_spec((1, d)), _const_spec((1, d)), _const_spec(wup.shape),
                  _const_spec(wdown.shape), _const_spec((1, d))],
        out_specs=pl.BlockSpec((tm, d), lambda s: (mlp_tile(s), 0)),
        out_shape=jax.ShapeDtypeStruct((n, d), _F32),
        scratch_shapes=[pltpu.VMEM((tm, aw), _BF16)],
        compiler_params=_params(),
        name="attn_mix_mlp",
    )(sinks, q2, k2, k2, v2, v2, x2, g_t, wglu_t, bglu.reshape(sw, 1), wout, gpost.reshape(1, d),
      gpre.reshape(1, d), wup, wdown, gmlp.reshape(1, d))


def _cmul(ar, ai, br, bi):
    return ar * br - ai * bi, ar * bi + ai * br


def _discretise(ar, ai, dt):
    mag = jnp.exp(ar * dt)
    ab_re, ab_im = mag * jnp.cos(ai * dt), mag * jnp.sin(ai * dt)
    den = ar * ar + ai * ai
    nr, ni = ab_re - 1.0, ab_im
    f_re = (nr * ar + ni * ai) / den
    f_im = (ni * ar - nr * ai) / den
    return ab_re, ab_im, f_re, f_im


def _pack(re, im):
    return jnp.concatenate([re, im], axis=1)


def _cmul_packed(a, s, s_swapped=None):
    p = a.shape[-1] // 2
    if s_swapped is None:
        s_swapped = pltpu.roll(s, p, axis=1)
    a_rr = jnp.concatenate([a[:, :p], a[:, :p]], axis=1)
    a_ii = jnp.concatenate([-a[:, p:], a[:, p:]], axis=1)
    return s * a_rr + s_swapped * a_ii


def _cpow_bits(base_re, base_im, tau, n_bits):
    out_re = out_im = None
    for bit in range(n_bits):
        on = ((tau >> bit) & 1) == 1
        f_re, f_im = jnp.where(on, base_re, 1.0), jnp.where(on, base_im, 0.0)
        out_re, out_im = (f_re, f_im) if out_re is None else _cmul(out_re, out_im, f_re, f_im)
        if bit + 1 < n_bits:
            base_re, base_im = _cmul(base_re, base_im, base_re, base_im)
    return out_re, out_im


def _ssm_prep_kernel(ldt_ref, a_re_ref, a_im_ref, bt_re_ref, bt_im_ref, c_re_ref, c_im_ref,
                     m_ref, bs_ref, ab_ref, apow_ref, bdb_ref, bdc_ref, kt_ref, *, straight_line):
    t_len, p_len, c_len = CHUNK, SSM_STATE, SSM_GROUP
    dt = jnp.exp(ldt_ref[...])
    ab_re, ab_im, f_re, f_im = _discretise(a_re_ref[...], a_im_ref[...], dt)
    ab = _pack(ab_re, ab_im)
    ab_ref[...] = ab
    bb_re, bb_im = _cmul(f_re, f_im, bt_re_ref[...], bt_im_ref[...])
    bb = _pack(bb_re, bb_im)

    row = lax.broadcasted_iota(jnp.int32, (t_len, 2 * p_len), 0)
    lo = lax.broadcasted_iota(jnp.int32, (t_len, 2 * p_len), 1) < p_len
    n_bits = (t_len - 1).bit_length()
    pw_re, pw_im = _cpow_bits(_pack(ab_re, ab_re), _pack(ab_im, ab_im),
                              jnp.where(lo, row, t_len - 1 - row), n_bits)
    pw_re_sw, pw_im_sw = pltpu.roll(pw_re, p_len, axis=1), pltpu.roll(pw_im, p_len, axis=1)
    w_fwd, w_fwd_sw = jnp.where(lo, pw_re, pw_im_sw), jnp.where(lo, pw_im, pw_re_sw)
    w_rev, w_rev_sw = jnp.where(lo, pw_re_sw, pw_im), jnp.where(lo, pw_im_sw, pw_re)
    w_next = _cmul_packed(ab, w_fwd, w_fwd_sw)
    w_next_sw = _cmul_packed(ab, w_fwd_sw, -w_fwd)
    st_re, st_im = ab_re, ab_im
    for _ in range(n_bits):
        st_re, st_im = _cmul(st_re, st_im, st_re, st_im)
    rows = []
    for _ in range(apow_ref.shape[0]):
        rows.append(_pack(st_re, st_im))
        st_re, st_im = _cmul(st_re, st_im, st_re, st_im)
    apow_ref[...] = jnp.concatenate(rows, axis=0)

    c_re, c_im = c_re_ref[...], c_im_ref[...]
    g_rows = []
    for cp in range(c_len):
        g_re, g_im = _cmul(bb_re[cp:cp + 1, :], bb_im[cp:cp + 1, :], c_re, c_im)
        g_rows.append(_pack(g_re, -g_im))
    g2 = jnp.concatenate(g_rows, axis=0)
    kt_ref[...] = lax.dot_general(g2, w_rev, (((1,), (1,)), ((), ())), preferred_element_type=_F32,
                                  precision=lax.Precision.HIGHEST)

    t_idx = lax.broadcasted_iota(jnp.int32, (t_len, t_len), 0)
    s_idx = lax.broadcasted_iota(jnp.int32, (t_len, t_len), 1)
    causal = s_idx <= t_idx

    def toeplitz_rows(c, carry):
        for cp in range(c_len):
            taps = kt_ref[pl.ds(cp * c_len + c, 1), :]
            full = jnp.broadcast_to(taps, (t_len, t_len))
            skew = pltpu.roll(full, 1, axis=1, stride=1, stride_axis=0)
            m_ref[pl.ds(pl.multiple_of(c * t_len, t_len), t_len), cp * t_len:(cp + 1) * t_len] = (
                jnp.where(causal, skew, 0.0).astype(m_ref.dtype))
        return carry

    if straight_line:
        for c in range(c_len):
            toeplitz_rows(c, 0)
    else:
        lax.fori_loop(0, c_len, toeplitz_rows, 0, unroll=2)

    for cp in range(c_len):
        bs_ref[cp * t_len:(cp + 1) * t_len, :] = _cmul_packed(bb[cp:cp + 1, :], w_rev, w_rev_sw).astype(bs_ref.dtype)

    lane = lax.broadcasted_iota(jnp.int32, (1, 2 * p_len), 1)
    conj = jnp.where(lane < p_len, 1.0, -1.0)
    c_pk = _pack(c_re, c_im)
    for c in range(c_len):
        m_ref[c * t_len:(c + 1) * t_len, c_len * t_len:] = (
            _cmul_packed(c_pk[c:c + 1, :], w_next, w_next_sw) * conj).astype(m_ref.dtype)

    n_state = bdb_ref.shape[1] // 2
    mine = lax.broadcasted_iota(jnp.int32, (c_len, n_state), 1) // p_len == pl.program_id(0)

    def spread(block):
        return jnp.where(mine, jnp.concatenate([block] * (n_state // p_len), axis=1), 0.0)

    bdb_ref[...] = _pack(spread(bb_re), spread(bb_im)).astype(bdb_ref.dtype)
    bdc_ref[...] = _pack(spread(c_re), spread(c_im)).astype(bdc_ref.dtype)


def _ssm_prep_parts(a_re, a_im, log_dt, b_re, b_im, c_re, c_im, n_pow, *, straight_line):
    g, p = a_re.shape
    c = b_re.shape[2]
    tw = c * CHUNK
    grp = lambda *shape: pl.BlockSpec((None,) + shape, lambda i: (i,) + (0,) * len(shape))
    rows = pl.BlockSpec((c, 2 * g * p), lambda i: (i, 0))
    return dict(
        kernel=functools.partial(_ssm_prep_kernel, straight_line=straight_line),
        operands=[log_dt.reshape(g, 1, 1), a_re.reshape(g, 1, p), a_im.reshape(g, 1, p),
                  jnp.swapaxes(b_re, 1, 2), jnp.swapaxes(b_im, 1, 2), c_re, c_im],
        in_specs=[grp(1, 1), grp(1, p), grp(1, p), grp(c, p), grp(c, p), grp(c, p), grp(c, p)],
        out_specs=[grp(tw, tw + 2 * p), grp(tw, 2 * p), grp(1, 2 * p), grp(n_pow, 2 * p), rows, rows],
        out_shape=[jax.ShapeDtypeStruct((g, tw, tw + 2 * p), _BF16),
                   jax.ShapeDtypeStruct((g, tw, 2 * p), _BF16),
                   jax.ShapeDtypeStruct((g, 1, 2 * p), _F32),
                   jax.ShapeDtypeStruct((g, n_pow, 2 * p), _F32),
                   jax.ShapeDtypeStruct((g * c, 2 * g * p), _BF16),
                   jax.ShapeDtypeStruct((g * c, 2 * g * p), _BF16)],
        scratch_shapes=[pltpu.VMEM((c * c, CHUNK), _F32)])


def _ssm_prompt_kernel(d_ref, u_ref, mt_ref, bs_ref, apow_ref, g_ref, st_ref, *, chunks_per_seq, row_block):
    n_grp = mt_ref.shape[0]
    c_len = u_ref.shape[0] // n_grp
    n_tok = u_ref.shape[1]
    nc = n_tok // CHUNK
    for gi in range(n_grp):
        grp = pl.program_id(0) * n_grp + gi
        u3 = u_ref[gi * c_len:(gi + 1) * c_len, :].reshape(c_len, nc, CHUNK)
        x = jnp.concatenate([u3[c] for c in range(c_len)], axis=1)
        s = jnp.dot(x, bs_ref[gi], preferred_element_type=_F32)
        k_idx = lax.broadcasted_iota(jnp.int32, s.shape, 0) % chunks_per_seq
        for j in range(apow_ref.shape[1]):
            d = 1 << j
            shifted = jnp.where(k_idx >= d, pltpu.roll(s, d, axis=0), 0.0)
            s = s + _cmul_packed(apow_ref[gi, j:j + 1, :], shifted)
        st_ref[gi] = s
        s_prev = jnp.where(k_idx >= 1, pltpu.roll(s, 1, axis=0), 0.0)
        xs_t = jnp.concatenate([u3[c].astype(_F32).T for c in range(c_len)] + [s_prev.T], axis=0).astype(_BF16)
        g_slabs = []
        for rb in range(c_len * CHUNK // row_block):
            y_t = jnp.dot(mt_ref[gi, rb * row_block:(rb + 1) * row_block, :], xs_t, preferred_element_type=_F32)
            for ci in range(row_block // CHUNK):
                c = rb * (row_block // CHUNK) + ci
                yc = y_t[ci * CHUNK:(ci + 1) * CHUNK, :].T + d_ref[grp * c_len + c] * u3[c].astype(_F32)
                g_slabs.append(jax.nn.gelu(yc).astype(g_ref.dtype))
        g_ref[gi * c_len:(gi + 1) * c_len, :] = jnp.stack(g_slabs, axis=0).reshape(c_len, n_tok)


def _ssm_prompt(u_t, d_skip, m, bs, apow, chunks_per_seq, *, groups_per_step):
    width, n = u_t.shape
    g, tw, kw = m.shape
    c = width // g
    nc = n // CHUNK
    p2 = bs.shape[2]
    n_pow = apow.shape[1]
    gps = groups_per_step
    grp = lambda *shape: pl.BlockSpec((gps,) + shape, lambda i: (i,) + (0,) * len(shape))
    return pl.pallas_call(
        functools.partial(_ssm_prompt_kernel, chunks_per_seq=chunks_per_seq, row_block=min(tw, 512)),
        grid=(g // gps,),
        in_specs=[pl.BlockSpec(memory_space=pltpu.SMEM),
                  pl.BlockSpec((gps * c, n), lambda i: (i, 0)),
                  grp(tw, kw), grp(tw, p2), grp(n_pow, p2)],
        out_specs=[pl.BlockSpec((gps * c, n), lambda i: (i, 0)), grp(nc, p2)],
        out_shape=[jax.ShapeDtypeStruct((width, n), _BF16),
                   jax.ShapeDtypeStruct((g, nc, p2), _F32)],
        compiler_params=_params(),
        name="ssm_prompt",
    )(d_skip, u_t, m, bs, apow)


def _mix_mlp_kernel(x_ref, att_ref, g_ref, wglu_ref, bglu_ref, wout_ref, gpost_ref, gpre_ref,
                    wup_ref, wdown_ref, gmlp_ref, o_ref, *, g_transposed, ff_chunk, sub_rows, filler=()):
    aw = att_ref.shape[1]
    d_ff = wup_ref.shape[1]

    def mix_stage(rows):
        if g_transposed:
            gact = g_ref[:, rows]
            z = jnp.dot(wglu_ref[...], gact, preferred_element_type=_F32) + bglu_ref[...]
        else:
            gact = g_ref[rows, :]
            z = jnp.dot(gact.astype(_BF16), wglu_ref[...], preferred_element_type=_F32) + bglu_ref[...]
        ssm = (gact.astype(_F32) * (1.0 / (1.0 + jnp.exp(-z)))).astype(_BF16)
        mix = jnp.dot(att_ref[rows, :], wout_ref[:aw, :], preferred_element_type=_F32)
        if g_transposed:
            mix = mix + lax.dot_general(ssm, wout_ref[aw:, :], (((0,), (0,)), ((), ())),
                                        preferred_element_type=_F32)
        else:
            mix = mix + jnp.dot(ssm, wout_ref[aw:, :], preferred_element_type=_F32)
        x1 = x_ref[rows, :] + _rms(mix, gpost_ref[...])
        return x1, _rms(x1, gpre_ref[...]).astype(_BF16)

    filler = list(filler)

    def mlp_stage(rows, x1, h):
        f = None
        for j in range(d_ff // ff_chunk):
            up = jnp.dot(h, wup_ref[:, j * ff_chunk:(j + 1) * ff_chunk], preferred_element_type=_F32)
            act = jnp.square(jnp.maximum(up, 0.0)).astype(_BF16)
            part = jnp.dot(act, wdown_ref[j * ff_chunk:(j + 1) * ff_chunk, :], preferred_element_type=_F32)
            f = part if f is None else f + part
            if filler:
                filler.pop(0)()
        o_ref[rows, :] = x1 + _rms(f, gmlp_ref[...])

    n_sub = x_ref.shape[0] // sub_rows
    row_slices = [slice(s * sub_rows, (s + 1) * sub_rows) for s in range(n_sub)]
    staged = mix_stage(row_slices[0])
    for s in range(n_sub):
        nxt = mix_stage(row_slices[s + 1]) if s + 1 < n_sub else None
        mlp_stage(row_slices[s], *staged)
        staged = nxt
    for item in filler:
        item()


def _mix_mlp(x2, att, gact, wglu, bglu, wout, gpost, gpre, wup, wdown, gmlp, *, tm, g_transposed):
    n, d = x2.shape
    aw = att.shape[1]
    sw = wglu.shape[0]
    d_ff = wup.shape[1]
    if g_transposed:
        g_spec = pl.BlockSpec((sw, tm), lambda i: (0, i))
        bglu2 = bglu.reshape(sw, 1)
    else:
        g_spec = pl.BlockSpec((tm, sw), lambda i: (i, 0))
        bglu2 = bglu.reshape(1, sw)
    return pl.pallas_call(
        functools.partial(_mix_mlp_kernel, g_transposed=g_transposed, ff_chunk=min(d_ff, 1024),
                          sub_rows=min(tm, 512)),
        grid=(n // tm,),
        in_specs=[pl.BlockSpec((tm, d), lambda i: (i, 0)), pl.BlockSpec((tm, aw), lambda i: (i, 0)), g_spec,
                  _const_spec(wglu.shape), _const_spec(bglu2.shape), _const_spec(wout.shape),
                  _const_spec((1, d)), _const_spec((1, d)), _const_spec(wup.shape),
                  _const_spec(wdown.shape), _const_spec((1, d))],
        out_specs=pl.BlockSpec((tm, d), lambda i: (i, 0)),
        out_shape=jax.ShapeDtypeStruct((n, d), _F32),
        compiler_params=_params(),
        name="mix_mlp_t" if g_transposed else "mix_mlp_n",
    )(x2, att, gact, wglu, bglu2, wout, gpost.reshape(1, d), gpre.reshape(1, d), wup, wdown,
      gmlp.reshape(1, d))


def _attn_sample_kernel(sink_ref, q_ref, kn_ref, vn_ref, kc_ref, vc_ref, o_ref, *, n_heads):
    steps, bb, _ = q_ref.shape
    win = kc_ref.shape[3]
    group = n_heads // N_KV_HEADS
    rows = group * steps * bb
    r_c = lax.broadcasted_iota(jnp.int32, (rows, bb * win), 0)
    c_c = lax.broadcasted_iota(jnp.int32, (rows, bb * win), 1)
    t_r, b_r = (r_c % (steps * bb)) // bb, r_c % bb
    mask_c = (c_c // win == b_r) & (c_c % win > t_r + win - WINDOW)
    r_n = lax.broadcasted_iota(jnp.int32, (rows, steps * bb), 0)
    c_n = lax.broadcasted_iota(jnp.int32, (rows, steps * bb), 1)
    t_rn, b_rn = (r_n % (steps * bb)) // bb, r_n % bb
    dt_n = t_rn - c_n // bb
    mask_n = (c_n % bb == b_rn) & (dt_n >= 0) & (dt_n < WINDOW)
    g_row = lax.broadcasted_iota(jnp.int32, (rows, 1), 0) // (steps * bb)
    scale = HEAD_DIM ** -0.5
    nt = (((1,), (1,)), ((), ()))
    q = q_ref[...].reshape(steps * bb, n_heads * HEAD_DIM)
    kn = kn_ref[...].reshape(steps * bb, N_KV_HEADS * HEAD_DIM)
    vn = vn_ref[...].reshape(steps * bb, N_KV_HEADS * HEAD_DIM)
    outs = []
    for kv in range(N_KV_HEADS):
        ls = slice(kv * HEAD_DIM, (kv + 1) * HEAD_DIM)
        qs = jnp.concatenate([q[:, (kv * group + gi) * HEAD_DIM:(kv * group + gi + 1) * HEAD_DIM]
                              for gi in range(group)], axis=0).astype(_BF16)
        sink = jnp.zeros((rows, 1), _F32)
        for gi in range(group):
            sink = jnp.where(g_row == gi, sink_ref[kv * group + gi], sink)
        kc_t = jnp.concatenate([kc_ref[b, kv] for b in range(bb)], axis=1).astype(_BF16)
        vc_t = jnp.concatenate([vc_ref[b, kv] for b in range(bb)], axis=1).astype(_BF16)
        s_c = jnp.dot(qs, kc_t, preferred_element_type=_F32) * scale
        s_n = lax.dot_general(qs, kn[:, ls].astype(_BF16), nt, preferred_element_type=_F32) * scale
        s_c = jnp.where(mask_c, s_c, NEG_INF)
        s_n = jnp.where(mask_n, s_n, NEG_INF)
        p_c, p_n = _softmax_sink([s_c, s_n], sink)
        o = (lax.dot_general(p_c.astype(_BF16), vc_t, nt, preferred_element_type=_F32)
             + jnp.dot(p_n.astype(_BF16), vn[:, ls].astype(_BF16), preferred_element_type=_F32))
        outs.extend(o[gi * steps * bb:(gi + 1) * steps * bb, :] for gi in range(group))
    o_ref[...] = jnp.concatenate(outs, axis=1).reshape(o_ref.shape).astype(o_ref.dtype)


def _attn_sample(q3, kn3, vn3, kc_t, vc_t, sinks, *, bb):
    steps, nb, aw = q3.shape
    kvw = kn3.shape[2]
    tok = lambda i: (0, i, 0)
    cache_spec = pl.BlockSpec((bb,) + kc_t.shape[1:], lambda i: (i, 0, 0, 0))
    return pl.pallas_call(
        functools.partial(_attn_sample_kernel, n_heads=aw // HEAD_DIM),
        grid=(nb // bb,),
        in_specs=[pl.BlockSpec(memory_space=pltpu.SMEM),
                  pl.BlockSpec((steps, bb, aw), tok), pl.BlockSpec((steps, bb, kvw), tok),
                  pl.BlockSpec((steps, bb, kvw), tok), cache_spec, cache_spec],
        out_specs=pl.BlockSpec((steps, bb, aw), tok),
        out_shape=jax.ShapeDtypeStruct((steps, nb, aw), _BF16),
        compiler_params=_params(),
        name="attn_sample",
    )(sinks, q3, kn3, vn3, kc_t, vc_t)


def _ssm_sample_kernel(u_ref, s0r_ref, s0i_ref, ab_ref, bdb_ref, bdc_ref, d_ref,
                       g_ref, sr_ref, si_ref, *, steps):
    nb = s0r_ref.shape[0]
    ns = s0r_ref.shape[1]
    nt = (((1,), (1,)), ((), ()))
    u = u_ref[...]
    bu = jnp.dot(u.astype(_BF16), bdb_ref[...], preferred_element_type=_F32)
    a_re, a_im = ab_ref[0:1, :], ab_ref[1:2, :]
    xr, xi = s0r_ref[...], s0i_ref[...]
    for t in range(steps):
        rs = slice(t * nb, (t + 1) * nb)
        nr = a_re * xr - a_im * xi + bu[rs, :ns]
        ni = a_re * xi + a_im * xr + bu[rs, ns:]
        xr, xi = nr, ni
        y = (lax.dot_general(xr.astype(_BF16), bdc_ref[:, :ns], nt, preferred_element_type=_F32)
             - lax.dot_general(xi.astype(_BF16), bdc_ref[:, ns:], nt, preferred_element_type=_F32))
        g_ref[rs, :] = jax.nn.gelu(y + d_ref[...] * u[rs, :])
    sr_ref[...] = xr
    si_ref[...] = xi


def _ssm_sample(u2, s0r, s0i, ab2, bd_b, bd_c, d_skip, *, steps):
    n, sw = u2.shape
    nb, ns = s0r.shape
    return pl.pallas_call(
        functools.partial(_ssm_sample_kernel, steps=steps),
        grid=(1,),
        in_specs=[_const_spec(u2.shape), _const_spec(s0r.shape), _const_spec(s0i.shape),
                  _const_spec(ab2.shape), _const_spec(bd_b.shape), _const_spec(bd_c.shape),
                  _const_spec((1, sw))],
        out_specs=[pl.BlockSpec((n, sw), lambda i: (0, 0)), pl.BlockSpec((nb, ns), lambda i: (0, 0)),
                   pl.BlockSpec((nb, ns), lambda i: (0, 0))],
        out_shape=[jax.ShapeDtypeStruct((n, sw), _F32), jax.ShapeDtypeStruct((nb, ns), _F32),
                   jax.ShapeDtypeStruct((nb, ns), _F32)],
        compiler_params=_params(),
        name="ssm_sample",
    )(u2, s0r, s0i, ab2, bd_b, bd_c, d_skip.reshape(1, sw))


def _tile_rows(n, target):
    tm = min(n, target)
    while n % tm:
        tm //= 2
    return tm


def _prompt_layer(x, w, ssm_weights, n_pow):
    b, l, d = x.shape
    n = b * l
    x2 = x.reshape(n, d)
    n_grp = ssm_weights[0].shape[0]
    tables = _rotary_tables(np.arange(l))
    tm = n // n_grp
    fuse = n % n_grp == 0 and tm % LANES == 0 and l % tm == 0 and tm <= 1024
    if not fuse:
        tm = _tile_rows(l, 1024)
    inproj = _inproj_parts(x2, w["g_mix_pre"], w["wqkv"], w["wu_t"], tables, tm=tm,
                           u_transposed=True, q_dtype=_BF16, u_dtype=_BF16)
    prep_parts = _ssm_prep_parts(*ssm_weights, n_pow, straight_line=fuse)
    if fuse:
        (q, k, v, u_t), prep = _call_fused(inproj, prep_parts, (n_grp,), "inproj_prep")
    else:
        q, k, v, u_t = _call_parts(inproj, (n // tm,), "inproj_t")
        prep = _call_parts(prep_parts, (n_grp,), "ssm_prep")
    aw, kvw = q.shape[1], k.shape[1]
    kc = l // CHUNK
    m, bs, _, apow, _, _ = prep
    g_t, states = _ssm_prompt(u_t, w["d_skip"], m, bs, apow, kc, groups_per_step=_tile_rows(m.shape[0], 2))
    y = _attn_mix_mlp(q, k, v, w["sinks"], x2, g_t, w["wglu_t"], w["b_glu"], w["wout"],
                      w["g_mix_post"], w["g_mlp_pre"], w["wup"], w["wdown"], w["g_mlp_post"],
                      tm=_tile_rows(l, 512), seq=l)
    keep = min(WINDOW, l)
    k_state = k.reshape(b, l, kvw)[:, l - keep:].reshape(b, keep, N_KV_HEADS, HEAD_DIM)
    v_state = v.reshape(b, l, kvw)[:, l - keep:].reshape(b, keep, N_KV_HEADS, HEAD_DIM)
    fin = states[:, kc - 1::kc, :]
    s_re = jnp.swapaxes(fin[:, :, :SSM_STATE], 0, 1)
    s_im = jnp.swapaxes(fin[:, :, SSM_STATE:], 0, 1)
    return (y.reshape(b, l, d), k_state, v_state, s_re, s_im), prep


def _sample_layer(x, cache_k, cache_v, s0_re, s0_im, w, prep):
    nb, steps, d = x.shape
    n = nb * steps
    xs = jnp.swapaxes(x, 0, 1).reshape(n, d)
    tables = tuple(np.repeat(t, nb, axis=0) for t in _rotary_tables(PAST_LEN + np.arange(steps)))
    q, k, v, u = _call_parts(_inproj_parts(xs, w["g_mix_pre"], w["wqkv"], w["wu"], tables, tm=n,
                                           u_transposed=False, q_dtype=_F32, u_dtype=_F32), (1,), "inproj_n")
    aw, kvw = q.shape[1], k.shape[1]
    win = cache_k.shape[1]
    att = _attn_sample(q.reshape(steps, nb, aw), k.reshape(steps, nb, kvw), v.reshape(steps, nb, kvw),
                       jnp.transpose(cache_k, (0, 2, 3, 1)), jnp.transpose(cache_v, (0, 2, 3, 1)),
                       w["sinks"], bb=_tile_rows(nb, 8))
    _, _, ab, _, bd_b, bd_c = prep
    p = SSM_STATE
    n_grp = ab.shape[0]
    ab2 = jnp.stack([ab[:, 0, :p].reshape(-1), ab[:, 0, p:].reshape(-1)])
    gact, s_re, s_im = _ssm_sample(u, s0_re.reshape(nb, n_grp * p), s0_im.reshape(nb, n_grp * p),
                                   ab2, bd_b, bd_c, w["d_skip"], steps=steps)
    y = _mix_mlp(xs, att.reshape(n, aw), gact, w["wglu"], w["b_glu"], w["wout"],
                 w["g_mix_post"], w["g_mlp_pre"], w["wup"], w["wdown"], w["g_mlp_post"],
                 tm=_tile_rows(n, 512), g_transposed=False)
    y = jnp.swapaxes(y.reshape(steps, nb, d), 0, 1)
    k_new = jnp.swapaxes(k.reshape(steps, nb, N_KV_HEADS, HEAD_DIM), 0, 1)
    v_new = jnp.swapaxes(v.reshape(steps, nb, N_KV_HEADS, HEAD_DIM), 0, 1)
    k_state = jnp.concatenate([cache_k, k_new], axis=1)[:, -win:]
    v_state = jnp.concatenate([cache_v, v_new], axis=1)[:, -win:]
    return y, k_state, v_state, s_re.reshape(nb, n_grp, p), s_im.reshape(nb, n_grp, p)


def kernel(x_prompt, x_sample, cache_k, cache_v, state_ssm_re, state_ssm_im, g_mix_pre, w_in, sinks,
           a_re, a_im, log_dt, b_re, b_im, c_re, c_im, d_skip, w_glu, b_glu, w_out, g_mix_post,
           g_mlp_pre, w_up, w_down, g_mlp_post):
    depth = w_in.shape[0]
    seq = x_prompt.shape[1]
    assert seq % CHUNK == 0 and seq % WINDOW == 0
    chunks_per_seq = seq // CHUNK
    n_pow = max(1, (chunks_per_seq - 1).bit_length())
    ssm_width = d_skip.shape[1]
    qkv_width = w_in.shape[2] - ssm_width
    yp, ys = x_prompt, x_sample
    outs = [[] for _ in range(8)]
    for l in range(depth):
        w = {
            "g_mix_pre": g_mix_pre[l], "sinks": sinks[l], "d_skip": d_skip[l], "b_glu": b_glu[l],
            "g_mix_post": g_mix_post[l], "g_mlp_pre": g_mlp_pre[l], "g_mlp_post": g_mlp_post[l],
            "c_re": c_re[l], "c_im": c_im[l],
            "wqkv": w_in[l, :, :qkv_width].astype(_BF16),
            "wu": w_in[l, :, qkv_width:].astype(_BF16),
            "wu_t": w_in[l, :, qkv_width:].T.astype(_BF16),
            "wglu": w_glu[l].astype(_BF16), "wglu_t": w_glu[l].T.astype(_BF16),
            "wout": w_out[l].astype(_BF16), "wup": w_up[l].astype(_BF16), "wdown": w_down[l].astype(_BF16),
        }
        (yp, kp, vp, srp, sip), prep = _prompt_layer(
            yp, w, (a_re[l], a_im[l], log_dt[l], b_re[l], b_im[l], c_re[l], c_im[l]), n_pow)
        ys, kss, vss, srs, sis = _sample_layer(ys, cache_k[l], cache_v[l], state_ssm_re[l],
                                               state_ssm_im[l], w, prep)
        for lst, val in zip(outs, (kp, vp, srp, sip, kss, vss, srs, sis)):
            lst.append(val)
    return (yp, ys) + tuple(jnp.stack(o) for o in outs)
```

```python
import functools

import jax
import jax.numpy as jnp
import numpy as np
from jax import lax
from jax.experimental import pallas as pl
from jax.experimental.pallas import tpu as pltpu

HEAD_DIM = 64
N_KV_HEADS = 2
WINDOW = 128
ROT_DIM = HEAD_DIM // 4
ROPE_THETA = 500000.0
SSM_GROUP = 16
SSM_STATE = 64
NORM_EPS = 1e-6
NEG_INF = -1e30
PAST_LEN = 16384
CHUNK = 128
LANES = 128
VMEM_LIMIT = 60 * 1024 * 1024

_F32 = jnp.float32
_BF16 = jnp.bfloat16


def _params(n_parallel=1):
    return pltpu.CompilerParams(dimension_semantics=("arbitrary",) * n_parallel,
                                vmem_limit_bytes=VMEM_LIMIT)


def _rms(x, g):
    return x * lax.rsqrt(jnp.mean(x * x, axis=-1, keepdims=True) + NORM_EPS) * g


def _const_spec(shape):
    nd = len(shape)
    return pl.BlockSpec(shape, lambda *_: (0,) * nd, pipeline_mode=pl.Buffered(1))


def _rotary_tables(pos):
    half = ROT_DIM // 2
    inv_freq = np.power(ROPE_THETA, -np.arange(half, dtype=np.float64) * (2.0 / ROT_DIM))
    ang = pos.astype(np.float64)[:, None] * inv_freq[None, :]
    cos, sin = np.cos(ang), np.sin(ang)
    n = pos.shape[0]
    ones = np.ones((n, HEAD_DIM - ROT_DIM))
    zeros_h = np.zeros((n, half))
    zeros_r = np.zeros((n, HEAD_DIM - ROT_DIM))
    c_tab = np.concatenate([cos, cos, ones], axis=1)
    s_up = np.concatenate([-sin, zeros_h, zeros_r], axis=1)
    s_dn = np.concatenate([zeros_h, sin, zeros_r], axis=1)
    rep = LANES // HEAD_DIM
    return tuple(np.tile(t, (1, rep)).astype(np.float32) for t in (c_tab, s_up, s_dn))


def _inproj_kernel(x_ref, g_ref, wqkv_ref, wu_ref, c_ref, su_ref, sd_ref,
                   q_ref, k_ref, v_ref, u_ref, *, attn_width, kv_width, u_transposed):
    half = ROT_DIM // 2
    h = _rms(x_ref[...], g_ref[...]).astype(_BF16)
    qkv = jnp.dot(h, wqkv_ref[...], preferred_element_type=_F32)
    c_tab, s_up, s_dn = c_ref[...], su_ref[...], sd_ref[...]

    def rot(blk):
        return (blk * c_tab + pltpu.roll(blk, LANES - half, axis=1) * s_up
                + pltpu.roll(blk, half, axis=1) * s_dn)

    for j in range(attn_width // LANES):
        q_ref[:, j * LANES:(j + 1) * LANES] = rot(qkv[:, j * LANES:(j + 1) * LANES]).astype(q_ref.dtype)
    for j in range(kv_width // LANES):
        o = attn_width + j * LANES
        k_ref[:, j * LANES:(j + 1) * LANES] = rot(qkv[:, o:o + LANES])
    v_ref[...] = qkv[:, attn_width + kv_width:]
    if u_transposed:
        u = lax.dot_general(wu_ref[...], h, (((1,), (1,)), ((), ())), preferred_element_type=_F32)
    else:
        u = jnp.dot(h, wu_ref[...], preferred_element_type=_F32)
    u_ref[...] = u.astype(u_ref.dtype)


def _inproj_parts(x2, g, wqkv, wu, tables, *, tm, u_transposed, q_dtype, u_dtype):
    n, d = x2.shape
    kv_width = N_KV_HEADS * HEAD_DIM
    attn_width = wqkv.shape[1] - 2 * kv_width
    ssm_width = wu.shape[0] if u_transposed else wu.shape[1]
    tab_blocks = tables[0].shape[0] // tm
    tab_spec = pl.BlockSpec((tm, LANES), lambda i: (i % tab_blocks, 0))
    if u_transposed:
        u_shape, u_spec = (ssm_width, n), pl.BlockSpec((ssm_width, tm), lambda i: (0, i))
    else:
        u_shape, u_spec = (n, ssm_width), pl.BlockSpec((tm, ssm_width), lambda i: (i, 0))
    return dict(
        kernel=functools.partial(_inproj_kernel, attn_width=attn_width, kv_width=kv_width,
                                 u_transposed=u_transposed),
        operands=[x2, g.reshape(1, d), wqkv, wu, *tables],
        in_specs=[pl.BlockSpec((tm, d), lambda i: (i, 0)), _const_spec((1, d)),
                  _const_spec(wqkv.shape), _const_spec(wu.shape), tab_spec, tab_spec, tab_spec],
        out_specs=[pl.BlockSpec((tm, attn_width), lambda i: (i, 0)),
                   pl.BlockSpec((tm, kv_width), lambda i: (i, 0)),
                   pl.BlockSpec((tm, kv_width), lambda i: (i, 0)), u_spec],
        out_shape=[jax.ShapeDtypeStruct((n, attn_width), q_dtype),
                   jax.ShapeDtypeStruct((n, kv_width), _F32),
                   jax.ShapeDtypeStruct((n, kv_width), _F32),
                   jax.ShapeDtypeStruct(u_shape, u_dtype)],
        scratch_shapes=[])


def _call_parts(parts, grid, name):
    return pl.pallas_call(parts["kernel"], grid=grid, in_specs=parts["in_specs"], out_specs=parts["out_specs"],
                          out_shape=parts["out_shape"], scratch_shapes=parts["scratch_shapes"],
                          compiler_params=_params(len(grid)), name=name)(*parts["operands"])


def _fused_kernel(*refs, first, second):
    a_in, a_out, a_scr = (len(first[k]) for k in ("in_specs", "out_specs", "scratch_shapes"))
    n_in = a_in + len(second["in_specs"])
    n_out = a_out + len(second["out_specs"])
    ins, outs, scr = refs[:n_in], refs[n_in:n_in + n_out], refs[n_in + n_out:]
    first["kernel"](*ins[:a_in], *outs[:a_out], *scr[:a_scr])
    second["kernel"](*ins[a_in:], *outs[a_out:], *scr[a_scr:])


def _call_fused(first, second, grid, name):
    outs = pl.pallas_call(
        functools.partial(_fused_kernel, first=first, second=second), grid=grid,
        in_specs=first["in_specs"] + second["in_specs"], out_specs=first["out_specs"] + second["out_specs"],
        out_shape=first["out_shape"] + second["out_shape"],
        scratch_shapes=first["scratch_shapes"] + second["scratch_shapes"],
        compiler_params=_params(len(grid)), name=name)(*first["operands"], *second["operands"])
    n_first = len(first["out_shape"])
    return outs[:n_first], outs[n_first:]


def _softmax_sink(s_parts, sink):
    m = sink
    for s in s_parts:
        m = jnp.maximum(m, jnp.max(s, axis=-1, keepdims=True))
    p_parts = [jnp.exp(s - m) for s in s_parts]
    denom = jnp.exp(sink - m)
    for p in p_parts:
        denom = denom + jnp.sum(p, axis=-1, keepdims=True)
    return [p / denom for p in p_parts]


def _attn_prompt_kernel(sink_ref, q_ref, kp_ref, kc_ref, vp_ref, vc_ref, o_ref, *, n_heads, first_step):
    blk = kp_ref.shape[0]
    n_sub = q_ref.shape[0] // blk
    heads_per_lane_block = LANES // HEAD_DIM
    kv_lane_blocks = n_heads // N_KV_HEADS // heads_per_lane_block
    row = lax.broadcasted_iota(jnp.int32, (blk, blk), 0)
    col = lax.broadcasted_iota(jnp.int32, (blk, blk), 1)
    own = col <= row
    lane = lax.broadcasted_iota(jnp.int32, (1, LANES), 1)
    scale = HEAD_DIM ** -0.5
    nt = (((1,), (1,)), ((), ()))
    k_all = jnp.concatenate([kp_ref[...], kc_ref[...]], axis=0)
    v_all = jnp.concatenate([vp_ref[...], vc_ref[...]], axis=0).astype(_BF16)
    k_swap = pltpu.roll(k_all, HEAD_DIM, axis=1)
    k_dup = [jnp.where((lane < HEAD_DIM) == (kv == 0), k_all, k_swap).astype(_BF16) for kv in range(N_KV_HEADS)]
    heads_per_kv = kv_lane_blocks * heads_per_lane_block

    def scores(sub):
        rows = slice(sub * blk, (sub + 1) * blk)
        keys = slice(sub * blk, (sub + 2) * blk)
        out = []
        for kv in range(N_KV_HEADS):
            q_rows = []
            for pr in range(kv_lane_blocks):
                j = kv * kv_lane_blocks + pr
                qp = q_ref[rows, j * LANES:(j + 1) * LANES] * scale
                q_rows.extend(jnp.where(lane // HEAD_DIM == e, qp, 0.0).astype(qp.dtype)
                              for e in range(heads_per_lane_block))
            out.append(lax.dot_general(jnp.concatenate(q_rows, axis=0), k_dup[kv][keys], nt,
                                       preferred_element_type=_F32))
        return out

    def finish(sub, s_all):
        rows = slice(sub * blk, (sub + 1) * blk)
        keys = slice(sub * blk, (sub + 2) * blk)
        valid = (own | jnp.logical_not(first_step)) if sub == 0 else None
        for kv in range(N_KV_HEADS):
            p_rows, inv_denoms = [], []
            s_kv = s_all[kv]
            for hi in range(heads_per_kv):
                sink = sink_ref[kv * heads_per_kv + hi]
                s_prev = s_kv[hi * blk:(hi + 1) * blk, :blk]
                s_own = s_kv[hi * blk:(hi + 1) * blk, blk:]
                s = jnp.where(own, s_own, s_prev)
                if valid is not None:
                    s = jnp.where(valid, s, NEG_INF)
                m = jnp.maximum(jnp.max(s, axis=-1, keepdims=True), sink)
                p = jnp.exp(s - m)
                denom = jnp.sum(p, axis=-1, keepdims=True) + jnp.exp(sink - m)
                p_rows.append(jnp.concatenate([jnp.where(own, 0.0, p), jnp.where(own, p, 0.0)],
                                              axis=1).astype(_BF16))
                inv_denoms.append(1.0 / denom)
            o_kv = jnp.dot(jnp.concatenate(p_rows, axis=0), v_all[keys], preferred_element_type=_F32)
            for pr in range(kv_lane_blocks):
                parts = []
                for e in range(heads_per_lane_block):
                    hi = pr * heads_per_lane_block + e
                    o_h = o_kv[hi * blk:(hi + 1) * blk, :] * inv_denoms[hi]
                    shift = (e - kv) * HEAD_DIM % LANES
                    parts.append(pltpu.roll(o_h, shift, axis=1) if shift else o_h)
                out = parts[0]
                for e in range(1, heads_per_lane_block):
                    out = jnp.where(lane >= e * HEAD_DIM, parts[e], out)
                j = kv * kv_lane_blocks + pr
                o_ref[rows, j * LANES:(j + 1) * LANES] = out.astype(o_ref.dtype)

    return [(functools.partial(scores, sub), functools.partial(finish, sub)) for sub in range(n_sub)]


def _attn_mix_kernel(sink_ref, q_ref, kp_ref, kc_ref, vp_ref, vc_ref,
                     x_ref, g_ref, wglu_ref, bglu_ref, wout_ref, gpost_ref, gpre_ref, wup_ref, wdown_ref, gmlp_ref,
                     o_ref, att_ref, *, n_heads, n_tiles, tiles_per_seq, ff_chunk, sub_rows):
    step = pl.program_id(0)

    @pl.when(step == 0)
    def _():
        att_ref[...] = jnp.zeros_like(att_ref)

    tile = jnp.minimum(step, n_tiles - 1)
    attn_blocks = _attn_prompt_kernel(sink_ref, q_ref, kp_ref, kc_ref, vp_ref, vc_ref, att_ref, n_heads=n_heads,
                                      first_step=tile % tiles_per_seq == 0)
    assert x_ref.shape[0] <= 2 * sub_rows
    _mix_mlp_kernel(x_ref, att_ref, g_ref, wglu_ref, bglu_ref, wout_ref, gpost_ref, gpre_ref,
                    wup_ref, wdown_ref, gmlp_ref, o_ref, g_transposed=True, ff_chunk=ff_chunk, sub_rows=sub_rows,
                    filler=attn_blocks)


def _attn_mix_mlp(q2, k2, v2, sinks, x2, g_t, wglu_t, bglu, wout, gpost, gpre, wup, wdown, gmlp, *, tm, seq):
    n, d = x2.shape
    aw, kvw = q2.shape[1], k2.shape[1]
    sw = wglu_t.shape[0]
    d_ff = wup.shape[1]
    n_tiles = n // tm
    blocks = tm // WINDOW
    att_tile = lambda s: jnp.minimum(s, n_tiles - 1)
    mlp_tile = lambda s: jnp.maximum(s - 1, 0)
    cur = lambda s: (att_tile(s), 0)
    prev = lambda s: (jnp.maximum(att_tile(s) * blocks - 1, 0), 0)
    return pl.pallas_call(
        functools.partial(_attn_mix_kernel, n_heads=aw // HEAD_DIM, n_tiles=n_tiles, tiles_per_seq=seq // tm,
                          ff_chunk=min(d_ff, 1024), sub_rows=min(tm, 512)),
        grid=(n_tiles + 1,),
        in_specs=[pl.BlockSpec(memory_space=pltpu.SMEM),
                  pl.BlockSpec((tm, aw), cur),
                  pl.BlockSpec((WINDOW, kvw), prev), pl.BlockSpec((tm, kvw), cur),
                  pl.BlockSpec((WINDOW, kvw), prev), pl.BlockSpec((tm, kvw), cur),
                  pl.BlockSpec((tm, d), lambda s: (mlp_tile(s), 0)),
                  pl.BlockSpec((sw, tm), lambda s: (0, mlp_tile(s))),
                  _const_spec(wglu_t.shape), _const_spec((sw, 1)), _const_spec(wout.shape),
                  _const_spec((1, d)), _const_spec((1, d)), _const_spec(wup.shape),
                  _const_spec(wdown.shape), _const_spec((1, d))],
        out_specs=pl.BlockSpec((tm, d), lambda s: (mlp_tile(s), 0)),
        out_shape=jax.ShapeDtypeStruct((n, d), _F32),
        scratch_shapes=[pltpu.VMEM((tm, aw), _BF16)],
        compiler_params=_params(),
        name="attn_mix_mlp",
    )(sinks, q2, k2, k2, v2, v2, x2, g_t, wglu_t, bglu.reshape(sw, 1), wout, gpost.reshape(1, d),
      gpre.reshape(1, d), wup, wdown, gmlp.reshape(1, d))


def _cmul(ar, ai, br, bi):
    return ar * br - ai * bi, ar * bi + ai * br


def _discretise(ar, ai, dt):
    mag = jnp.exp(ar * dt)
    ab_re, ab_im = mag * jnp.cos(ai * dt), mag * jnp.sin(ai * dt)
    den = ar * ar + ai * ai
    nr, ni = ab_re - 1.0, ab_im
    f_re = (nr * ar + ni * ai) / den
    f_im = (ni * ar - nr * ai) / den
    return ab_re, ab_im, f_re, f_im


def _pack(re, im):
    return jnp.concatenate([re, im], axis=1)


def _cmul_packed(a, s, s_swapped=None):
    p = a.shape[-1] // 2
    if s_swapped is None:
        s_swapped = pltpu.roll(s, p, axis=1)
    a_rr = jnp.concatenate([a[:, :p], a[:, :p]], axis=1)
    a_ii = jnp.concatenate([-a[:, p:], a[:, p:]], axis=1)
    return s * a_rr + s_swapped * a_ii


def _cpow_bits(base_re, base_im, tau, n_bits):
    out_re = out_im = None
    for bit in range(n_bits):
        on = ((tau >> bit) & 1) == 1
        f_re, f_im = jnp.where(on, base_re, 1.0), jnp.where(on, base_im, 0.0)
        out_re, out_im = (f_re, f_im) if out_re is None else _cmul(out_re, out_im, f_re, f_im)
        if bit + 1 < n_bits:
            base_re, base_im = _cmul(base_re, base_im, base_re, base_im)
    return out_re, out_im


def _ssm_prep_kernel(ldt_ref, a_re_ref, a_im_ref, bt_re_ref, bt_im_ref, c_re_ref, c_im_ref,
                     m_ref, bs_ref, ab_ref, apow_ref, bdb_ref, bdc_ref, kt_ref, *, straight_line):
    t_len, p_len, c_len = CHUNK, SSM_STATE, SSM_GROUP
    dt = jnp.exp(ldt_ref[...])
    ab_re, ab_im, f_re, f_im = _discretise(a_re_ref[...], a_im_ref[...], dt)
    ab = _pack(ab_re, ab_im)
    ab_ref[...] = ab
    bb_re, bb_im = _cmul(f_re, f_im, bt_re_ref[...], bt_im_ref[...])
    bb = _pack(bb_re, bb_im)

    row = lax.broadcasted_iota(jnp.int32, (t_len, 2 * p_len), 0)
    lo = lax.broadcasted_iota(jnp.int32, (t_len, 2 * p_len), 1) < p_len
    n_bits = (t_len - 1).bit_length()
    pw_re, pw_im = _cpow_bits(_pack(ab_re, ab_re), _pack(ab_im, ab_im),
                              jnp.where(lo, row, t_len - 1 - row), n_bits)
    pw_re_sw, pw_im_sw = pltpu.roll(pw_re, p_len, axis=1), pltpu.roll(pw_im, p_len, axis=1)
    w_fwd, w_fwd_sw = jnp.where(lo, pw_re, pw_im_sw), jnp.where(lo, pw_im, pw_re_sw)
    w_rev, w_rev_sw = jnp.where(lo, pw_re_sw, pw_im), jnp.where(lo, pw_im_sw, pw_re)
    w_next = _cmul_packed(ab, w_fwd, w_fwd_sw)
    w_next_sw = _cmul_packed(ab, w_fwd_sw, -w_fwd)
    st_re, st_im = ab_re, ab_im
    for _ in range(n_bits):
        st_re, st_im = _cmul(st_re, st_im, st_re, st_im)
    rows = []
    for _ in range(apow_ref.shape[0]):
        rows.append(_pack(st_re, st_im))
        st_re, st_im = _cmul(st_re, st_im, st_re, st_im)
    apow_ref[...] = jnp.concatenate(rows, axis=0)

    c_re, c_im = c_re_ref[...], c_im_ref[...]
    g_rows = []
    for cp in range(c_len):
        g_re, g_im = _cmul(bb_re[cp:cp + 1, :], bb_im[cp:cp + 1, :], c_re, c_im)
        g_rows.append(_pack(g_re, -g_im))
    g2 = jnp.concatenate(g_rows, axis=0)
    kt_ref[...] = lax.dot_general(g2, w_rev, (((1,), (1,)), ((), ())), preferred_element_type=_F32,
                                  precision=lax.Precision.HIGHEST)

    t_idx = lax.broadcasted_iota(jnp.int32, (t_len, t_len), 0)
    s_idx = lax.broadcasted_iota(jnp.int32, (t_len, t_len), 1)
    causal = s_idx <= t_idx

    def toeplitz_rows(c, carry):
        for cp in range(c_len):
            taps = kt_ref[pl.ds(cp * c_len + c, 1), :]
            full = jnp.broadcast_to(taps, (t_len, t_len))
            skew = pltpu.roll(full, 1, axis=1, stride=1, stride_axis=0)
            m_ref[pl.ds(pl.multiple_of(c * t_len, t_len), t_len), cp * t_len:(cp + 1) * t_len] = (
                jnp.where(causal, skew, 0.0).astype(m_ref.dtype))
        return carry

    if straight_line:
        for c in range(c_len):
            toeplitz_rows(c, 0)
    else:
        lax.fori_loop(0, c_len, toeplitz_rows, 0, unroll=2)

    for cp in range(c_len):
        bs_ref[cp * t_len:(cp + 1) * t_len, :] = _cmul_packed(bb[cp:cp + 1, :], w_rev, w_rev_sw).astype(bs_ref.dtype)

    lane = lax.broadcasted_iota(jnp.int32, (1, 2 * p_len), 1)
    conj = jnp.where(lane < p_len, 1.0, -1.0)
    c_pk = _pack(c_re, c_im)
    for c in range(c_len):
        m_ref[c * t_len:(c + 1) * t_len, c_len * t_len:] = (
            _cmul_packed(c_pk[c:c + 1, :], w_next, w_next_sw) * conj).astype(m_ref.dtype)

    n_state = bdb_ref.shape[1] // 2
    mine = lax.broadcasted_iota(jnp.int32, (c_len, n_state), 1) // p_len == pl.program_id(0)

    def spread(block):
        return jnp.where(mine, jnp.concatenate([block] * (n_state // p_len), axis=1), 0.0)

    bdb_ref[...] = _pack(spread(bb_re), spread(bb_im)).astype(bdb_ref.dtype)
    bdc_ref[...] = _pack(spread(c_re), spread(c_im)).astype(bdc_ref.dtype)


def _ssm_prep_parts(a_re, a_im, log_dt, b_re, b_im, c_re, c_im, n_pow, *, straight_line):
    g, p = a_re.shape
    c = b_re.shape[2]
    tw = c * CHUNK
    grp = lambda *shape: pl.BlockSpec((None,) + shape, lambda i: (i,) + (0,) * len(shape))
    rows = pl.BlockSpec((c, 2 * g * p), lambda i: (i, 0))
    return dict(
        kernel=functools.partial(_ssm_prep_kernel, straight_line=straight_line),
        operands=[log_dt.reshape(g, 1, 1), a_re.reshape(g, 1, p), a_im.reshape(g, 1, p),
                  jnp.swapaxes(b_re, 1, 2), jnp.swapaxes(b_im, 1, 2), c_re, c_im],
        in_specs=[grp(1, 1), grp(1, p), grp(1, p), grp(c, p), grp(c, p), grp(c, p), grp(c, p)],
        out_specs=[grp(tw, tw + 2 * p), grp(tw, 2 * p), grp(1, 2 * p), grp(n_pow, 2 * p), rows, rows],
        out_shape=[jax.ShapeDtypeStruct((g, tw, tw + 2 * p), _BF16),
                   jax.ShapeDtypeStruct((g, tw, 2 * p), _BF16),
                   jax.ShapeDtypeStruct((g, 1, 2 * p), _F32),
                   jax.ShapeDtypeStruct((g, n_pow, 2 * p), _F32),
                   jax.ShapeDtypeStruct((g * c, 2 * g * p), _BF16),
                   jax.ShapeDtypeStruct((g * c, 2 * g * p), _BF16)],
        scratch_shapes=[pltpu.VMEM((c * c, CHUNK), _F32)])


def _ssm_prompt_kernel(d_ref, u_ref, mt_ref, bs_ref, apow_ref, g_ref, st_ref, *, chunks_per_seq, row_block):
    n_grp = mt_ref.shape[0]
    c_len = u_ref.shape[0] // n_grp
    n_tok = u_ref.shape[1]
    nc = n_tok // CHUNK
    for gi in range(n_grp):
        grp = pl.program_id(0) * n_grp + gi
        u3 = u_ref[gi * c_len:(gi + 1) * c_len, :].reshape(c_len, nc, CHUNK)
        x = jnp.concatenate([u3[c] for c in range(c_len)], axis=1)
        s = jnp.dot(x, bs_ref[gi], preferred_element_type=_F32)
        k_idx = lax.broadcasted_iota(jnp.int32, s.shape, 0) % chunks_per_seq
        for j in range(apow_ref.shape[1]):
            d = 1 << j
            shifted = jnp.where(k_idx >= d, pltpu.roll(s, d, axis=0), 0.0)
            s = s + _cmul_packed(apow_ref[gi, j:j + 1, :], shifted)
        st_ref[gi] = s
        s_prev = jnp.where(k_idx >= 1, pltpu.roll(s, 1, axis=0), 0.0)
        xs_t = jnp.concatenate([u3[c].astype(_F32).T for c in range(c_len)] + [s_prev.T], axis=0).astype(_BF16)
        g_slabs = []
        for rb in range(c_len * CHUNK // row_block):
            y_t = jnp.dot(mt_ref[gi, rb * row_block:(rb + 1) * row_block, :], xs_t, preferred_element_type=_F32)
            for ci in range(row_block // CHUNK):
                c = rb * (row_block // CHUNK) + ci
                yc = y_t[ci * CHUNK:(ci + 1) * CHUNK, :].T + d_ref[grp * c_len + c] * u3[c].astype(_F32)
                g_slabs.append(jax.nn.gelu(yc).astype(g_ref.dtype))
        g_ref[gi * c_len:(gi + 1) * c_len, :] = jnp.stack(g_slabs, axis=0).reshape(c_len, n_tok)


def _ssm_prompt(u_t, d_skip, m, bs, apow, chunks_per_seq, *, groups_per_step):
    width, n = u_t.shape
    g, tw, kw = m.shape
    c = width // g
    nc = n // CHUNK
    p2 = bs.shape[2]
    n_pow = apow.shape[1]
    gps = groups_per_step
    grp = lambda *shape: pl.BlockSpec((gps,) + shape, lambda i: (i,) + (0,) * len(shape))
    return pl.pallas_call(
        functools.partial(_ssm_prompt_kernel, chunks_per_seq=chunks_per_seq, row_block=min(tw, 512)),
        grid=(g // gps,),
        in_specs=[pl.BlockSpec(memory_space=pltpu.SMEM),
                  pl.BlockSpec((gps * c, n), lambda i: (i, 0)),
                  grp(tw, kw), grp(tw, p2), grp(n_pow, p2)],
        out_specs=[pl.BlockSpec((gps * c, n), lambda i: (i, 0)), grp(nc, p2)],
        out_shape=[jax.ShapeDtypeStruct((width, n), _BF16),
                   jax.ShapeDtypeStruct((g, nc, p2), _F32)],
        compiler_params=_params(),
        name="ssm_prompt",
    )(d_skip, u_t, m, bs, apow)


def _mix_mlp_kernel(x_ref, att_ref, g_ref, wglu_ref, bglu_ref, wout_ref, gpost_ref, gpre_ref,
                    wup_ref, wdown_ref, gmlp_ref, o_ref, *, g_transposed, ff_chunk, sub_rows, filler=()):
    aw = att_ref.shape[1]
    d_ff = wup_ref.shape[1]

    def mix_stage(rows):
        if g_transposed:
            gact = g_ref[:, rows]
            z = jnp.dot(wglu_ref[...], gact, preferred_element_type=_F32) + bglu_ref[...]
        else:
            gact = g_ref[rows, :]
            z = jnp.dot(gact.astype(_BF16), wglu_ref[...], preferred_element_type=_F32) + bglu_ref[...]
        ssm = (gact.astype(_F32) * (1.0 / (1.0 + jnp.exp(-z)))).astype(_BF16)
        mix = jnp.dot(att_ref[rows, :], wout_ref[:aw, :], preferred_element_type=_F32)
        if g_transposed:
            mix = mix + lax.dot_general(ssm, wout_ref[aw:, :], (((0,), (0,)), ((), ())),
                                        preferred_element_type=_F32)
        else:
            mix = mix + jnp.dot(ssm, wout_ref[aw:, :], preferred_element_type=_F32)
        x1 = x_ref[rows, :] + _rms(mix, gpost_ref[...])
        return x1, _rms(x1, gpre_ref[...]).astype(_BF16)

    filler = list(filler)

    def mlp_stage(rows, x1, h):
        f = None
        for j in range(d_ff // ff_chunk):
            start, finish = filler.pop(0) if filler else (None, None)
            started = start() if start else None
            up = jnp.dot(h, wup_ref[:, j * ff_chunk:(j + 1) * ff_chunk], preferred_element_type=_F32)
            if finish:
                finish(started)
            act = jnp.square(jnp.maximum(up, 0.0)).astype(_BF16)
            part = jnp.dot(act, wdown_ref[j * ff_chunk:(j + 1) * ff_chunk, :], preferred_element_type=_F32)
            f = part if f is None else f + part
        o_ref[rows, :] = x1 + _rms(f, gmlp_ref[...])

    n_sub = x_ref.shape[0] // sub_rows
    row_slices = [slice(s * sub_rows, (s + 1) * sub_rows) for s in range(n_sub)]
    staged = mix_stage(row_slices[0])
    for s in range(n_sub):
        nxt = mix_stage(row_slices[s + 1]) if s + 1 < n_sub else None
        mlp_stage(row_slices[s], *staged)
        staged = nxt
    for start, finish in filler:
        finish(start())


def _mix_mlp(x2, att, gact, wglu, bglu, wout, gpost, gpre, wup, wdown, gmlp, *, tm, g_transposed):
    n, d = x2.shape
    aw = att.shape[1]
    sw = wglu.shape[0]
    d_ff = wup.shape[1]
    if g_transposed:
        g_spec = pl.BlockSpec((sw, tm), lambda i: (0, i))
        bglu2 = bglu.reshape(sw, 1)
    else:
        g_spec = pl.BlockSpec((tm, sw), lambda i: (i, 0))
        bglu2 = bglu.reshape(1, sw)
    return pl.pallas_call(
        functools.partial(_mix_mlp_kernel, g_transposed=g_transposed, ff_chunk=min(d_ff, 1024),
                          sub_rows=min(tm, 512)),
        grid=(n // tm,),
        in_specs=[pl.BlockSpec((tm, d), lambda i: (i, 0)), pl.BlockSpec((tm, aw), lambda i: (i, 0)), g_spec,
                  _const_spec(wglu.shape), _const_spec(bglu2.shape), _const_spec(wout.shape),
                  _const_spec((1, d)), _const_spec((1, d)), _const_spec(wup.shape),
                  _const_spec(wdown.shape), _const_spec((1, d))],
        out_specs=pl.BlockSpec((tm, d), lambda i: (i, 0)),
        out_shape=jax.ShapeDtypeStruct((n, d), _F32),
        compiler_params=_params(),
        name="mix_mlp_t" if g_transposed else "mix_mlp_n",
    )(x2, att, gact, wglu, bglu2, wout, gpost.reshape(1, d), gpre.reshape(1, d), wup, wdown,
      gmlp.reshape(1, d))


def _attn_sample_kernel(sink_ref, q_ref, kn_ref, vn_ref, kc_ref, vc_ref, o_ref, *, n_heads):
    steps, bb, _ = q_ref.shape
    win = kc_ref.shape[3]
    group = n_heads // N_KV_HEADS
    rows = group * steps * bb
    r_c = lax.broadcasted_iota(jnp.int32, (rows, bb * win), 0)
    c_c = lax.broadcasted_iota(jnp.int32, (rows, bb * win), 1)
    t_r, b_r = (r_c % (steps * bb)) // bb, r_c % bb
    mask_c = (c_c // win == b_r) & (c_c % win > t_r + win - WINDOW)
    r_n = lax.broadcasted_iota(jnp.int32, (rows, steps * bb), 0)
    c_n = lax.broadcasted_iota(jnp.int32, (rows, steps * bb), 1)
    t_rn, b_rn = (r_n % (steps * bb)) // bb, r_n % bb
    dt_n = t_rn - c_n // bb
    mask_n = (c_n % bb == b_rn) & (dt_n >= 0) & (dt_n < WINDOW)
    g_row = lax.broadcasted_iota(jnp.int32, (rows, 1), 0) // (steps * bb)
    scale = HEAD_DIM ** -0.5
    nt = (((1,), (1,)), ((), ()))
    q = q_ref[...].reshape(steps * bb, n_heads * HEAD_DIM)
    kn = kn_ref[...].reshape(steps * bb, N_KV_HEADS * HEAD_DIM)
    vn = vn_ref[...].reshape(steps * bb, N_KV_HEADS * HEAD_DIM)
    outs = []
    for kv in range(N_KV_HEADS):
        ls = slice(kv * HEAD_DIM, (kv + 1) * HEAD_DIM)
        qs = jnp.concatenate([q[:, (kv * group + gi) * HEAD_DIM:(kv * group + gi + 1) * HEAD_DIM]
                              for gi in range(group)], axis=0).astype(_BF16)
        sink = jnp.zeros((rows, 1), _F32)
        for gi in range(group):
            sink = jnp.where(g_row == gi, sink_ref[kv * group + gi], sink)
        kc_t = jnp.concatenate([kc_ref[b, kv] for b in range(bb)], axis=1).astype(_BF16)
        vc_t = jnp.concatenate([vc_ref[b, kv] for b in range(bb)], axis=1).astype(_BF16)
        s_c = jnp.dot(qs, kc_t, preferred_element_type=_F32) * scale
        s_n = lax.dot_general(qs, kn[:, ls].astype(_BF16), nt, preferred_element_type=_F32) * scale
        s_c = jnp.where(mask_c, s_c, NEG_INF)
        s_n = jnp.where(mask_n, s_n, NEG_INF)
        p_c, p_n = _softmax_sink([s_c, s_n], sink)
        o = (lax.dot_general(p_c.astype(_BF16), vc_t, nt, preferred_element_type=_F32)
             + jnp.dot(p_n.astype(_BF16), vn[:, ls].astype(_BF16), preferred_element_type=_F32))
        outs.extend(o[gi * steps * bb:(gi + 1) * steps * bb, :] for gi in range(group))
    o_ref[...] = jnp.concatenate(outs, axis=1).reshape(o_ref.shape).astype(o_ref.dtype)


def _attn_sample(q3, kn3, vn3, kc_t, vc_t, sinks, *, bb):
    steps, nb, aw = q3.shape
    kvw = kn3.shape[2]
    tok = lambda i: (0, i, 0)
    cache_spec = pl.BlockSpec((bb,) + kc_t.shape[1:], lambda i: (i, 0, 0, 0))
    return pl.pallas_call(
        functools.partial(_attn_sample_kernel, n_heads=aw // HEAD_DIM),
        grid=(nb // bb,),
        in_specs=[pl.BlockSpec(memory_space=pltpu.SMEM),
                  pl.BlockSpec((steps, bb, aw), tok), pl.BlockSpec((steps, bb, kvw), tok),
                  pl.BlockSpec((steps, bb, kvw), tok), cache_spec, cache_spec],
        out_specs=pl.BlockSpec((steps, bb, aw), tok),
        out_shape=jax.ShapeDtypeStruct((steps, nb, aw), _BF16),
        compiler_params=_params(),
        name="attn_sample",
    )(sinks, q3, kn3, vn3, kc_t, vc_t)


def _ssm_sample_kernel(u_ref, s0r_ref, s0i_ref, ab_ref, bdb_ref, bdc_ref, d_ref,
                       g_ref, sr_ref, si_ref, *, steps):
    nb = s0r_ref.shape[0]
    ns = s0r_ref.shape[1]
    nt = (((1,), (1,)), ((), ()))
    u = u_ref[...]
    bu = jnp.dot(u.astype(_BF16), bdb_ref[...], preferred_element_type=_F32)
    a_re, a_im = ab_ref[0:1, :], ab_ref[1:2, :]
    xr, xi = s0r_ref[...], s0i_ref[...]
    for t in range(steps):
        rs = slice(t * nb, (t + 1) * nb)
        nr = a_re * xr - a_im * xi + bu[rs, :ns]
        ni = a_re * xi + a_im * xr + bu[rs, ns:]
        xr, xi = nr, ni
        y = (lax.dot_general(xr.astype(_BF16), bdc_ref[:, :ns], nt, preferred_element_type=_F32)
             - lax.dot_general(xi.astype(_BF16), bdc_ref[:, ns:], nt, preferred_element_type=_F32))
        g_ref[rs, :] = jax.nn.gelu(y + d_ref[...] * u[rs, :])
    sr_ref[...] = xr
    si_ref[...] = xi


def _ssm_sample(u2, s0r, s0i, ab2, bd_b, bd_c, d_skip, *, steps):
    n, sw = u2.shape
    nb, ns = s0r.shape
    return pl.pallas_call(
        functools.partial(_ssm_sample_kernel, steps=steps),
        grid=(1,),
        in_specs=[_const_spec(u2.shape), _const_spec(s0r.shape), _const_spec(s0i.shape),
                  _const_spec(ab2.shape), _const_spec(bd_b.shape), _const_spec(bd_c.shape),
                  _const_spec((1, sw))],
        out_specs=[pl.BlockSpec((n, sw), lambda i: (0, 0)), pl.BlockSpec((nb, ns), lambda i: (0, 0)),
                   pl.BlockSpec((nb, ns), lambda i: (0, 0))],
        out_shape=[jax.ShapeDtypeStruct((n, sw), _F32), jax.ShapeDtypeStruct((nb, ns), _F32),
                   jax.ShapeDtypeStruct((nb, ns), _F32)],
        compiler_params=_params(),
        name="ssm_sample",
    )(u2, s0r, s0i, ab2, bd_b, bd_c, d_skip.reshape(1, sw))


def _tile_rows(n, target):
    tm = min(n, target)
    while n % tm:
        tm //= 2
    return tm


def _prompt_layer(x, w, ssm_weights, n_pow):
    b, l, d = x.shape
    n = b * l
    x2 = x.reshape(n, d)
    n_grp = ssm_weights[0].shape[0]
    tables = _rotary_tables(np.arange(l))
    tm = n // n_grp
    fuse = n % n_grp == 0 and tm % LANES == 0 and l % tm == 0 and tm <= 1024
    if not fuse:
        tm = _tile_rows(l, 1024)
    inproj = _inproj_parts(x2, w["g_mix_pre"], w["wqkv"], w["wu_t"], tables, tm=tm,
                           u_transposed=True, q_dtype=_BF16, u_dtype=_BF16)
    prep_parts = _ssm_prep_parts(*ssm_weights, n_pow, straight_line=fuse)
    if fuse:
        (q, k, v, u_t), prep = _call_fused(inproj, prep_parts, (n_grp,), "inproj_prep")
    else:
        q, k, v, u_t = _call_parts(inproj, (n // tm,), "inproj_t")
        prep = _call_parts(prep_parts, (n_grp,), "ssm_prep")
    aw, kvw = q.shape[1], k.shape[1]
    kc = l // CHUNK
    m, bs, _, apow, _, _ = prep
    g_t, states = _ssm_prompt(u_t, w["d_skip"], m, bs, apow, kc, groups_per_step=_tile_rows(m.shape[0], 2))
    y = _attn_mix_mlp(q, k, v, w["sinks"], x2, g_t, w["wglu_t"], w["b_glu"], w["wout"],
                      w["g_mix_post"], w["g_mlp_pre"], w["wup"], w["wdown"], w["g_mlp_post"],
                      tm=_tile_rows(l, 512), seq=l)
    keep = min(WINDOW, l)
    k_state = k.reshape(b, l, kvw)[:, l - keep:].reshape(b, keep, N_KV_HEADS, HEAD_DIM)
    v_state = v.reshape(b, l, kvw)[:, l - keep:].reshape(b, keep, N_KV_HEADS, HEAD_DIM)
    fin = states[:, kc - 1::kc, :]
    s_re = jnp.swapaxes(fin[:, :, :SSM_STATE], 0, 1)
    s_im = jnp.swapaxes(fin[:, :, SSM_STATE:], 0, 1)
    return (y.reshape(b, l, d), k_state, v_state, s_re, s_im), prep


def _sample_layer(x, cache_k, cache_v, s0_re, s0_im, w, prep):
    nb, steps, d = x.shape
    n = nb * steps
    xs = jnp.swapaxes(x, 0, 1).reshape(n, d)
    tables = tuple(np.repeat(t, nb, axis=0) for t in _rotary_tables(PAST_LEN + np.arange(steps)))
    q, k, v, u = _call_parts(_inproj_parts(xs, w["g_mix_pre"], w["wqkv"], w["wu"], tables, tm=n,
                                           u_transposed=False, q_dtype=_F32, u_dtype=_F32), (1,), "inproj_n")
    aw, kvw = q.shape[1], k.shape[1]
    win = cache_k.shape[1]
    att = _attn_sample(q.reshape(steps, nb, aw), k.reshape(steps, nb, kvw), v.reshape(steps, nb, kvw),
                       jnp.transpose(cache_k, (0, 2, 3, 1)), jnp.transpose(cache_v, (0, 2, 3, 1)),
                       w["sinks"], bb=_tile_rows(nb, 8))
    _, _, ab, _, bd_b, bd_c = prep
    p = SSM_STATE
    n_grp = ab.shape[0]
    ab2 = jnp.stack([ab[:, 0, :p].reshape(-1), ab[:, 0, p:].reshape(-1)])
    gact, s_re, s_im = _ssm_sample(u, s0_re.reshape(nb, n_grp * p), s0_im.reshape(nb, n_grp * p),
                                   ab2, bd_b, bd_c, w["d_skip"], steps=steps)
    y = _mix_mlp(xs, att.reshape(n, aw), gact, w["wglu"], w["b_glu"], w["wout"],
                 w["g_mix_post"], w["g_mlp_pre"], w["wup"], w["wdown"], w["g_mlp_post"],
                 tm=_tile_rows(n, 512), g_transposed=False)
    y = jnp.swapaxes(y.reshape(steps, nb, d), 0, 1)
    k_new = jnp.swapaxes(k.reshape(steps, nb, N_KV_HEADS, HEAD_DIM), 0, 1)
    v_new = jnp.swapaxes(v.reshape(steps, nb, N_KV_HEADS, HEAD_DIM), 0, 1)
    k_state = jnp.concatenate([cache_k, k_new], axis=1)[:, -win:]
    v_state = jnp.concatenate([cache_v, v_new], axis=1)[:, -win:]
    return y, k_state, v_state, s_re.reshape(nb, n_grp, p), s_im.reshape(nb, n_grp, p)


def kernel(x_prompt, x_sample, cache_k, cache_v, state_ssm_re, state_ssm_im, g_mix_pre, w_in, sinks,
           a_re, a_im, log_dt, b_re, b_im, c_re, c_im, d_skip, w_glu, b_glu, w_out, g_mix_post,
           g_mlp_pre, w_up, w_down, g_mlp_post):
    depth = w_in.shape[0]
    seq = x_prompt.shape[1]
    assert seq % CHUNK == 0 and seq % WINDOW == 0
    chunks_per_seq = seq // CHUNK
    n_pow = max(1, (chunks_per_seq - 1).bit_length())
    ssm_width = d_skip.shape[1]
    qkv_width = w_in.shape[2] - ssm_width
    yp, ys = x_prompt, x_sample
    outs = [[] for _ in range(8)]
    for l in range(depth):
        w = {
            "g_mix_pre": g_mix_pre[l], "sinks": sinks[l], "d_skip": d_skip[l], "b_glu": b_glu[l],
            "g_mix_post": g_mix_post[l], "g_mlp_pre": g_mlp_pre[l], "g_mlp_post": g_mlp_post[l],
            "c_re": c_re[l], "c_im": c_im[l],
            "wqkv": w_in[l, :, :qkv_width].astype(_BF16),
            "wu": w_in[l, :, qkv_width:].astype(_BF16),
            "wu_t": w_in[l, :, qkv_width:].T.astype(_BF16),
            "wglu": w_glu[l].astype(_BF16), "wglu_t": w_glu[l].T.astype(_BF16),
            "wout": w_out[l].astype(_BF16), "wup": w_up[l].astype(_BF16), "wdown": w_down[l].astype(_BF16),
        }
        (yp, kp, vp, srp, sip), prep = _prompt_layer(
            yp, w, (a_re[l], a_im[l], log_dt[l], b_re[l], b_im[l], c_re[l], c_im[l]), n_pow)
        ys, kss, vss, srs, sis = _sample_layer(ys, cache_k[l], cache_v[l], state_ssm_re[l],
                                               state_ssm_im[l], w, prep)
        for lst, val in zip(outs, (kp, vp, srp, sip, kss, vss, srs, sis)):
            lst.append(val)
    return (yp, ys) + tuple(jnp.stack(o) for o in outs)
```
